```python
import math
import jax, jax.numpy as jnp
from jax import lax
import numpy as np

D_MODEL = 1024
BATCH = 8
SEQ = 8192
DEPTH = 2

CHUNK = 64
N_META = 16
N_MIXERS = 2
EPS = 1e-6
GLA_HEADS = 4
GLA_KEY = D_MODEL // 2
GLA_DK = GLA_KEY // GLA_HEADS
GLA_DV = D_MODEL // GLA_HEADS
GLA_GATE_RANK = 16
GLA_TAU = 16.0
GLA_IN = 2 * GLA_KEY + 2 * D_MODEL
ATT_HEADS = 8
ATT_HEAD_DIM = D_MODEL // ATT_HEADS
KV_LATENT = 256
IDX_HEADS = 8
IDX_DIM = 64
TOPK_MAX = 256
Q_BLOCK = 128
DSA_IN = ATT_HEADS * ATT_HEAD_DIM + KV_LATENT + IDX_HEADS * IDX_DIM + IDX_DIM + IDX_HEADS
REL_BUCKETS = 32
REL_MAX_DIST = 128
D_FF = -(-(8 * D_MODEL) // (3 * 256)) * 256
N_GLA = (DEPTH + 1) // 2
N_DSA = DEPTH // 2

kernel_name = "hybrid_gla_dsa_streaming_trunk"


def rmsnorm(x, g):
    xf = x.astype(jnp.float32)
    y = xf * lax.rsqrt(jnp.mean(xf * xf, axis=-1, keepdims=True) + EPS) * g.astype(jnp.float32)
    return y.astype(x.dtype)


def chunk_ids(pos):
    return jnp.where(pos < N_META, 0, 1 + (pos - N_META) // CHUNK)


def rel_bucket(rel):
    nb = REL_BUCKETS // 2
    max_exact = nb // 2
    ret = jnp.where(rel > 0, nb, 0)
    n = jnp.abs(rel)
    nf = jnp.maximum(n, 1).astype(jnp.float32)
    large = max_exact + (jnp.log(nf / max_exact) / math.log(REL_MAX_DIST / max_exact)
                         * (nb - max_exact)).astype(jnp.int32)
    large = jnp.minimum(large, nb - 1)
    return ret + jnp.where(n < max_exact, n, large)


def gla_mixer(h, w_in, w_a1, w_a2, b_a, g_out, w_out):
    B, T, _ = h.shape
    proj = h @ w_in
    q, k, v, r = jnp.split(proj, [GLA_KEY, 2 * GLA_KEY, 2 * GLA_KEY + D_MODEL], axis=-1)
    log_a = jax.nn.log_sigmoid((h @ w_a1) @ w_a2 + b_a).astype(jnp.float32) / GLA_TAU
    n_pad = CHUNK - N_META
    Tp = T + n_pad
    nc = Tp // CHUNK

    def to_chunks(a, d):
        a = jnp.pad(a, ((0, 0), (n_pad, 0), (0, 0)))
        return a.reshape(B, nc, CHUNK, GLA_HEADS, d).transpose(1, 0, 3, 2, 4)

    qc = to_chunks(q * GLA_DK ** -0.5, GLA_DK)
    kc = to_chunks(k, GLA_DK)
    vc = to_chunks(v, GLA_DV)
    gc = to_chunks(log_a, GLA_DK)

    def step(S, inp):
        qi, ki, vi, gi = inp
        cum = jnp.cumsum(gi, axis=2)
        tot = cum[:, :, -1:, :]
        S = jnp.exp(tot[:, :, 0, :, None]) * S + jnp.einsum(
            'bhcd,bhce->bhde', ki.astype(jnp.float32) * jnp.exp(tot - cum), vi.astype(jnp.float32))
        o = jnp.einsum('bhcd,bhde->bhce', qi.astype(jnp.float32), S)
        return S, o

    S0 = jnp.zeros((B, GLA_HEADS, GLA_DK, GLA_DV), jnp.float32)
    _, o = lax.scan(step, S0, (qc, kc, vc, gc))
    o = o.transpose(1, 0, 3, 2, 4).reshape(B, Tp, GLA_HEADS, GLA_DV)[:, n_pad:]
    o = rmsnorm(o, g_out.reshape(GLA_HEADS, GLA_DV)).astype(h.dtype).reshape(B, T, D_MODEL)
    return (o * jax.nn.silu(r)) @ w_out


def dsa_mixer(h, w_in, g_kv, w_uk, w_uv, w_out, rel_bias):
    B, T, _ = h.shape
    s1 = ATT_HEADS * ATT_HEAD_DIM
    s2 = s1 + KV_LATENT
    s3 = s2 + IDX_HEADS * IDX_DIM
    s4 = s3 + IDX_DIM
    proj = h @ w_in
    q, c, q_idx, k_idx, w_idx = jnp.split(proj, [s1, s2, s3, s4], axis=-1)
    c = rmsnorm(c, g_kv)
    q = q.reshape(B, T, ATT_HEADS, ATT_HEAD_DIM)
    q_idx = q_idx.reshape(B, T, IDX_HEADS, IDX_DIM)
    w_idx = w_idx * IDX_HEADS ** -0.5
    n_sel = min(TOPK_MAX, SEQ // 4)
    nblk = -(-T // Q_BLOCK)
    Tq = nblk * Q_BLOCK

    def blocks(a):
        a = jnp.pad(a, [(0, 0), (0, Tq - T)] + [(0, 0)] * (a.ndim - 2))
        return jnp.moveaxis(a.reshape(B, nblk, Q_BLOCK, *a.shape[2:]), 1, 0)

    key_chunk = chunk_ids(jnp.arange(T))

    def attend(args):
        qb, qib, wb, start = args
        qpos = start + jnp.arange(Q_BLOCK)
        q_chunk = chunk_ids(qpos)
        s_idx = jnp.einsum('bqhd,bsd->bqhs', qib, k_idx) * IDX_DIM ** -0.5
        score = jnp.einsum('bqhs,bqh->bqs', jax.nn.relu(s_idx), wb).astype(jnp.float32)
        admissible = key_chunk[None, :] <= q_chunk[:, None]
        score = jnp.where(admissible[None], score, -jnp.inf)
        top_val, top_idx = lax.top_k(score, n_sel)
        valid = jnp.isfinite(top_val)
        c_sel = jax.vmap(lambda cb, ib: cb[ib])(c, top_idx)
        q_lat = jnp.einsum('bqhd,hcd->bqhc', qb, w_uk)
        logits = jnp.einsum('bqhc,bqkc->bhqk', q_lat, c_sel).astype(jnp.float32) * ATT_HEAD_DIM ** -0.5
        bias = rel_bias[rel_bucket(top_idx - qpos[None, :, None])]
        logits = logits + jnp.moveaxis(bias, -1, 1).astype(jnp.float32)
        logits = jnp.where(valid[:, None], logits, -jnp.inf)
        p = jax.nn.softmax(logits, axis=-1).astype(c.dtype)
        u = jnp.einsum('bhqk,bqkc->bqhc', p, c_sel)
        return jnp.einsum('bqhc,hcd->bqhd', u, w_uv)

    starts = jnp.arange(nblk, dtype=jnp.int32) * Q_BLOCK
    o = lax.map(attend, (blocks(q), blocks(q_idx), blocks(w_idx), starts))
    o = jnp.moveaxis(o, 0, 1).reshape(B, Tq, ATT_HEADS * ATT_HEAD_DIM)[:, :T]
    return o @ w_out


def swiglu(h, w_in, w_out):
    gate, up = jnp.split(h @ w_in, 2, axis=-1)
    return (jax.nn.silu(gate) * up) @ w_out


def setup_inputs(seed: int = 0) -> dict:
    key = jax.random.key(seed)
    ks = jax.random.split(key, 20)
    nrm = lambda k, shape, fan_in: jax.random.normal(k, shape, jnp.float32) * fan_in ** -0.5
    gain = lambda k, shape: 1.0 + 0.05 * jax.random.normal(k, shape, jnp.float32)
    return {
        "x": jax.random.normal(ks[0], (BATCH, SEQ, D_MODEL), jnp.float32),
        "meta": jax.random.normal(ks[1], (N_META, D_MODEL), jnp.float32),
        "norm_mix": gain(ks[2], (DEPTH, D_MODEL)),
        "norm_ffn": gain(ks[3], (DEPTH, D_MODEL)),
        "norm_final": gain(ks[4], (D_MODEL,)),
        "gla_w_in": nrm(ks[5], (N_GLA, D_MODEL, GLA_IN), D_MODEL),
        "gla_w_a1": nrm(ks[6], (N_GLA, D_MODEL, GLA_GATE_RANK), D_MODEL),
        "gla_w_a2": nrm(ks[7], (N_GLA, GLA_GATE_RANK, GLA_KEY), GLA_GATE_RANK),
        "gla_b_a": 0.1 * jax.random.normal(ks[8], (N_GLA, GLA_KEY), jnp.float32),
        "gla_g_out": gain(ks[9], (N_GLA, D_MODEL)),
        "gla_w_out": nrm(ks[10], (N_GLA, D_MODEL, D_MODEL), D_MODEL),
        "dsa_w_in": nrm(ks[11], (N_DSA, D_MODEL, DSA_IN), D_MODEL),
        "dsa_g_kv": gain(ks[12], (N_DSA, KV_LATENT)),
        "dsa_w_uk": nrm(ks[13], (N_DSA, ATT_HEADS, KV_LATENT, ATT_HEAD_DIM), KV_LATENT),
        "dsa_w_uv": nrm(ks[14], (N_DSA, ATT_HEADS, KV_LATENT, ATT_HEAD_DIM), KV_LATENT),
        "dsa_w_out": nrm(ks[15], (N_DSA, ATT_HEADS * ATT_HEAD_DIM, D_MODEL), ATT_HEADS * ATT_HEAD_DIM),
        "rel_bias": 0.5 * jax.random.normal(ks[16], (REL_BUCKETS, ATT_HEADS), jnp.float32),
        "ffn_w_in": nrm(ks[17], (DEPTH, D_MODEL, 2 * D_FF), D_MODEL),
        "ffn_w_out": nrm(ks[18], (DEPTH, D_FF, D_MODEL), D_FF),
    }


def reference(x, meta, norm_mix, norm_ffn, norm_final, gla_w_in, gla_w_a1, gla_w_a2, gla_b_a,
              gla_g_out, gla_w_out, dsa_w_in, dsa_g_kv, dsa_w_uk, dsa_w_uv, dsa_w_out, rel_bias,
              ffn_w_in, ffn_w_out):
    B = x.shape[0]
    h = jnp.concatenate(
        [jnp.broadcast_to(meta[None].astype(x.dtype), (B, N_META, D_MODEL)), x], axis=1)
    for i in range(DEPTH):
        j = i // N_MIXERS
        hn = rmsnorm(h, norm_mix[i])
        if i % N_MIXERS == 0:
            h = h + gla_mixer(hn, gla_w_in[j], gla_w_a1[j], gla_w_a2[j], gla_b_a[j],
                              gla_g_out[j], gla_w_out[j])
        else:
            h = h + dsa_mixer(hn, dsa_w_in[j], dsa_g_kv[j], dsa_w_uk[j], dsa_w_uv[j],
                              dsa_w_out[j], rel_bias)
        h = h + swiglu(rmsnorm(h, norm_ffn[i]), ffn_w_in[i], ffn_w_out[i])
    return rmsnorm(h, norm_final)[:, N_META:]
```

```python
import functools
import math

import jax
import jax.numpy as jnp
from jax import lax
from jax.experimental import pallas as pl
from jax.experimental.pallas import tpu as pltpu

F32 = jnp.float32
BF16 = jnp.bfloat16
I32 = jnp.int32

EPS = 1e-6
CHUNK = 64
N_META = 16
GLA_HEADS = 4
GLA_TAU = 16.0
GLA_GATE_RANK = 16
ATT_HEADS = 8
IDX_HEADS = 8
IDX_DIM = 64
KV_LATENT = 256
TOPK_MAX = 256
REL_BUCKETS = 32
REL_MAX_DIST = 128

LANES = 128
SUBLANES = 8
META_ROWS = 128
QB = 256
VMEM_LIMIT_BYTES = 56 * 1024 * 1024

INT_MIN = -(2 ** 31)
NEG_BIG = -1e30

_NT = (((1,), (1,)), ((), ()))
_TN = (((0,), (0,)), ((), ()))


def _params(n_grid):
    return pltpu.CompilerParams(
        dimension_semantics=("arbitrary",) * n_grid,
        vmem_limit_bytes=VMEM_LIMIT_BYTES)


def _full(shape):
    nd = len(shape)
    return pl.BlockSpec(shape, lambda *_: (0,) * nd, pipeline_mode=pl.Buffered(1))


def _rms(x, gain):
    return x * lax.rsqrt(jnp.mean(x * x, axis=-1, keepdims=True) + EPS) * gain


def _silu(x):
    return x / (1.0 + jnp.exp(-x))


def _gla_in_kernel(x_ref, gn_ref, w_ref, wa1_ref, wa2_ref, ba_ref,
                   q_ref, k_ref, v_ref, r_ref, g_ref, *, key_dim, d_model, q_scale):
    hb = _rms(x_ref[0], gn_ref[...]).astype(BF16)
    o = 0
    q = jnp.dot(hb, w_ref[:, o:o + key_dim], preferred_element_type=F32)
    q_ref[0] = (q * q_scale).astype(BF16)
    o += key_dim
    k_ref[0] = jnp.dot(hb, w_ref[:, o:o + key_dim], preferred_element_type=F32).astype(BF16)
    o += key_dim
    v_ref[0] = jnp.dot(hb, w_ref[:, o:o + d_model], preferred_element_type=F32).astype(BF16)
    o += d_model
    r_ref[0] = jnp.dot(hb, w_ref[:, o:o + d_model], preferred_element_type=F32).astype(BF16)
    a = jnp.dot(hb, wa1_ref[...], preferred_element_type=F32).astype(BF16)
    z = jnp.dot(a, wa2_ref[...], preferred_element_type=F32) + ba_ref[...]
    log_sig = jnp.minimum(z, 0.0) - jnp.log(1.0 + jnp.exp(-jnp.abs(z)))
    g_ref[0] = log_sig * (1.0 / GLA_TAU)


def _gla_in(x, gn, w, wa1, wa2, ba, *, rows):
    b, s, d = x.shape
    key_dim = d // 2
    dk = key_dim // GLA_HEADS
    kern = functools.partial(_gla_in_kernel, key_dim=key_dim, d_model=d, q_scale=dk ** -0.5)
    row = lambda width: pl.BlockSpec((1, rows, width), lambda i, j: (i, j, 0))
    return pl.pallas_call(
        kern,
        grid=(b, s // rows),
        in_specs=[row(d), _full(gn.shape), _full(w.shape), _full(wa1.shape),
                  _full(wa2.shape), _full(ba.shape)],
        out_specs=[row(key_dim), row(key_dim), row(d), row(d), row(key_dim)],
        out_shape=[jax.ShapeDtypeStruct((b, s, key_dim), BF16),
                   jax.ShapeDtypeStruct((b, s, key_dim), BF16),
                   jax.ShapeDtypeStruct((b, s, d), BF16),
                   jax.ShapeDtypeStruct((b, s, d), BF16),
                   jax.ShapeDtypeStruct((b, s, key_dim), F32)],
        compiler_params=_params(2),
        name="gla_in",
    )(x, gn, w, wa1, wa2, ba)


def _gla_scan_kernel(q_ref, k_ref, v_ref, r_ref, g_ref, gout_ref, s0_ref,
                     o_ref, sfin_ref, s_scr, *, rows, n_valid, dk, dv):
    j = pl.program_id(1)

    @pl.when(j == 0)
    def _():
        s_scr[...] = s0_ref[...]

    ri = lax.broadcasted_iota(I32, (CHUNK, CHUNK), 0)
    ci = lax.broadcasted_iota(I32, (CHUNK, CHUNK), 1)
    tri = (ri >= ci).astype(F32)
    for c in range(rows // CHUNK):
        rs = slice(c * CHUNK, (c + 1) * CHUNK)
        for h in range(GLA_HEADS):
            ks = slice(h * dk, (h + 1) * dk)
            vs = slice(h * dv, (h + 1) * dv)
            g = g_ref[0, rs, ks]
            if n_valid is not None:
                row = j * rows + c * CHUNK + lax.broadcasted_iota(I32, (CHUNK, dk), 0)
                g = jnp.where(row < n_valid, g, 0.0)
            cum = jnp.dot(tri, g, preferred_element_type=F32, precision=lax.Precision.HIGHEST)
            tot = cum[CHUNK - 1:CHUNK, :]
            kd = (k_ref[0, rs, ks].astype(F32) * jnp.exp(tot - cum)).astype(BF16)
            upd = lax.dot_general(v_ref[0, rs, vs], kd, _TN, preferred_element_type=F32)
            s_new = jnp.exp(tot) * s_scr[h] + upd
            s_scr[h] = s_new
            o = lax.dot_general(q_ref[0, rs, ks], s_new.astype(BF16), _NT,
                                preferred_element_type=F32)
            on = _rms(o, gout_ref[:, vs])
            rr = r_ref[0, rs, vs].astype(F32)
            o_ref[0, rs, vs] = (on * _silu(rr)).astype(BF16)

    @pl.when(j == pl.num_programs(1) - 1)
    def _():
        sfin_ref[0] = s_scr[...]


def _gla_scan(q, k, v, r, g, gout, s0, *, rows, n_valid):
    b, s, d = v.shape
    key_dim = q.shape[-1]
    dk, dv = key_dim // GLA_HEADS, d // GLA_HEADS
    kern = functools.partial(_gla_scan_kernel, rows=rows, n_valid=n_valid, dk=dk, dv=dv)
    row = lambda width: pl.BlockSpec((1, rows, width), lambda i, j: (i, j, 0))
    return pl.pallas_call(
        kern,
        grid=(b, s // rows),
        in_specs=[row(key_dim), row(key_dim), row(d), row(d), row(key_dim),
                  _full(gout.shape), _full(s0.shape)],
        out_specs=[row(d), pl.BlockSpec((1, GLA_HEADS, dv, dk), lambda i, j: (i, 0, 0, 0))],
        out_shape=[jax.ShapeDtypeStruct((b, s, d), BF16),
                   jax.ShapeDtypeStruct((b, GLA_HEADS, dv, dk), F32)],
        scratch_shapes=[pltpu.VMEM((GLA_HEADS, dv, dk), F32)],
        compiler_params=_params(2),
        name="gla_scan",
    )(q, k, v, r, g, gout, s0)


def _proj_residual_kernel(h_ref, a_ref, w_ref, o_ref):
    o_ref[0] = h_ref[0] + jnp.dot(a_ref[0], w_ref[...], preferred_element_type=F32)


def _proj_residual(h, a, w, *, rows):
    b, s, d = h.shape
    row = lambda width: pl.BlockSpec((1, rows, width), lambda i, j: (i, j, 0))
    return pl.pallas_call(
        _proj_residual_kernel,
        grid=(b, s // rows),
        in_specs=[row(d), row(a.shape[-1]), _full(w.shape)],
        out_specs=row(d),
        out_shape=jax.ShapeDtypeStruct((b, s, d), F32),
        compiler_params=_params(2),
        name="proj_residual",
    )(h, a, w)


def _ffn_kernel(h_ref, gn_ref, wg_ref, wu_ref, wo_ref, gf_ref, o_ref, hn_scr, acc_scr,
                *, final_norm):
    t = pl.program_id(2)

    @pl.when(t == 0)
    def _():
        hn_scr[...] = _rms(h_ref[0], gn_ref[...]).astype(BF16)
        acc_scr[...] = jnp.zeros_like(acc_scr)

    hb = hn_scr[...]
    gate = jnp.dot(hb, wg_ref[...], preferred_element_type=F32)
    up = jnp.dot(hb, wu_ref[...], preferred_element_type=F32)
    act = (_silu(gate) * up).astype(BF16)
    acc_scr[...] += jnp.dot(act, wo_ref[...], preferred_element_type=F32)

    @pl.when(t == pl.num_programs(2) - 1)
    def _():
        y = h_ref[0] + acc_scr[...]
        if final_norm:
            y = _rms(y, gf_ref[...])
        o_ref[0] = y


def _ffn(h, gn, w_in, w_out, gf, *, rows, ff_tile, final_norm):
    b, s, d = h.shape
    d_ff = w_out.shape[0]
    n_t = d_ff // ff_tile
    kern = functools.partial(_ffn_kernel, final_norm=final_norm)
    row = pl.BlockSpec((1, rows, d), lambda i, j, t: (i, j, 0))
    return pl.pallas_call(
        kern,
        grid=(b, s // rows, n_t),
        in_specs=[row, _full(gn.shape),
                  pl.BlockSpec((d, ff_tile), lambda i, j, t: (0, t)),
                  pl.BlockSpec((d, ff_tile), lambda i, j, t: (0, n_t + t)),
                  pl.BlockSpec((ff_tile, d), lambda i, j, t: (t, 0)),
                  _full(gf.shape)],
        out_specs=row,
        out_shape=jax.ShapeDtypeStruct((b, s, d), F32),
        scratch_shapes=[pltpu.VMEM((rows, d), BF16), pltpu.VMEM((rows, d), F32)],
        compiler_params=_params(3),
        name="ffn",
    )(h, gn, w_in, w_in, w_out, gf)


def _dsa_in_kernel(h_ref, gn_ref, wqt_ref, wuk_ref, wc_ref, wct_ref, gkv_ref, gkvc_ref,
                   wqit_ref, wk_ref, wwt_ref,
                   qlt_ref, qit_ref, c_ref, ct_ref, k_ref, wt_ref, *, rb, head_dim):
    hb = _rms(h_ref[0], gn_ref[...]).astype(BF16)
    qt = lax.dot_general(wqt_ref[...], hb, _NT, preferred_element_type=F32)
    for h in range(ATT_HEADS):
        qh = qt[h * head_dim:(h + 1) * head_dim, :].astype(BF16)
        ql = jnp.dot(wuk_ref[h], qh, preferred_element_type=F32)
        qlt_ref[0, 0, :, h * rb:(h + 1) * rb] = (ql * head_dim ** -0.5).astype(BF16)
    qit = lax.dot_general(wqit_ref[...], hb, _NT, preferred_element_type=F32)
    qit = (qit * IDX_DIM ** -0.5).astype(BF16)
    for h in range(IDX_HEADS):
        qit_ref[0, 0, :, h * rb:(h + 1) * rb] = qit[h * IDX_DIM:(h + 1) * IDX_DIM, :]
    c = jnp.dot(hb, wc_ref[...], preferred_element_type=F32)
    c_ref[0, 0] = _rms(c, gkv_ref[...]).astype(BF16)
    ct = lax.dot_general(wct_ref[...], hb, _NT, preferred_element_type=F32)
    ct = ct * lax.rsqrt(jnp.mean(ct * ct, axis=0, keepdims=True) + EPS) * gkvc_ref[...]
    ct_ref[0, 0] = ct.astype(BF16)
    k_ref[0, 0] = jnp.dot(hb, wk_ref[...], preferred_element_type=F32).astype(BF16)
    wt = lax.dot_general(wwt_ref[...], hb, _NT, preferred_element_type=F32)
    wt_ref[0, 0] = wt * IDX_HEADS ** -0.5


def _dsa_in(h, gn, wqt, wuk, wc, wct, gkv, gkvc, wqit, wk, wwt, *, rb):
    b, s, d = h.shape
    nb = s // rb
    head_dim = wuk.shape[-1]
    kern = functools.partial(_dsa_in_kernel, rb=rb, head_dim=head_dim)
    blk = lambda r, c: pl.BlockSpec((1, 1, r, c), lambda i, j: (i, j, 0, 0))
    shp = lambda r, c, dt: jax.ShapeDtypeStruct((b, nb, r, c), dt)
    return pl.pallas_call(
        kern,
        grid=(b, nb),
        in_specs=[pl.BlockSpec((1, rb, d), lambda i, j: (i, j, 0))] +
                 [_full(a.shape) for a in (gn, wqt, wuk, wc, wct, gkv, gkvc, wqit, wk, wwt)],
        out_specs=[blk(KV_LATENT, ATT_HEADS * rb), blk(IDX_DIM, IDX_HEADS * rb),
                   blk(rb, KV_LATENT), blk(KV_LATENT, rb), blk(rb, IDX_DIM),
                   blk(IDX_HEADS, rb)],
        out_shape=[shp(KV_LATENT, ATT_HEADS * rb, BF16), shp(IDX_DIM, IDX_HEADS * rb, BF16),
                   shp(rb, KV_LATENT, BF16), shp(KV_LATENT, rb, BF16), shp(rb, IDX_DIM, BF16),
                   shp(IDX_HEADS, rb, F32)],
        compiler_params=_params(2),
        name="dsa_in",
    )(h, gn, wqt, wuk, wc, wct, gkv, gkvc, wqit, wk, wwt)


def _sort_key(score):
    score = jnp.where(score == 0.0, 0.0, score)
    bits = pltpu.bitcast(score, I32)
    return bits ^ ((bits >> 31) & 0x7FFFFFFF)


def _dsa_core_kernel(h_ref, qlt_ref, qit_ref, wt_ref, k_ref, c_ref, ct_ref,
                     km_ref, cm_ref, ctm_ref, bcur_ref, bprev_ref, bmeta_ref, bfar_ref,
                     wuvt_ref, wout_ref, o_ref,
                     keys_scr, keysm_scr, m_scr, l_scr, acc_scr, *, n_sel, head_dim):
    j = pl.program_id(1)
    nh = ATT_HEADS

    def scores(k_blk):
        s8 = jnp.dot(k_blk, qit_ref[0, 0], preferred_element_type=F32)
        acc = None
        for h in range(IDX_HEADS):
            term = jnp.maximum(s8[:, h * QB:(h + 1) * QB], 0.0) * wt_ref[0, 0, h:h + 1, :]
            acc = term if acc is None else acc + term
        return _sort_key(acc)

    def score_body(kb, carry):
        keys_scr[kb] = scores(k_ref[0, kb])
        return carry

    lax.fori_loop(0, j, score_body, 0)
    s_chunk = lax.broadcasted_iota(I32, (QB, QB), 0) // CHUNK
    q_chunk = lax.broadcasted_iota(I32, (QB, QB), 1) // CHUNK
    keys_scr[j] = jnp.where(s_chunk <= q_chunk, scores(k_ref[0, j]), INT_MIN)
    m_row = lax.broadcasted_iota(I32, (META_ROWS, QB), 0)
    keysm_scr[...] = jnp.where(m_row < N_META, scores(km_ref[0, 0]), INT_MIN)

    groups = QB // SUBLANES

    def count_block(load, n_groups, cand8, accs):
        accs = list(accs)
        for r in range(n_groups):
            blk = load(r)
            accs[r % len(accs)] = accs[r % len(accs)] + jnp.where(blk >= cand8, 1, 0)
        return tuple(accs)

    def bit_body(i, res):
        cand_u = res | jnp.left_shift(jnp.int32(1), 31 - i)
        cand8 = jnp.broadcast_to(cand_u ^ INT_MIN, (SUBLANES, QB))
        zero = jnp.zeros((SUBLANES, QB), I32)

        def kb_body(kb, accs):
            return count_block(lambda r: keys_scr[kb, r * SUBLANES:(r + 1) * SUBLANES, :],
                               groups, cand8, accs)

        accs = lax.fori_loop(0, j + 1, kb_body, (zero, zero, zero, zero))
        accs = count_block(lambda r: keysm_scr[r * SUBLANES:(r + 1) * SUBLANES, :],
                           META_ROWS // SUBLANES, cand8, accs)
        cnt = jnp.sum(accs[0] + accs[1] + accs[2] + accs[3], axis=0, keepdims=True)
        return jnp.where(cnt >= n_sel, cand_u, res)

    res = lax.fori_loop(0, 32, bit_body, jnp.zeros((1, QB), I32))
    tau = jnp.maximum(res ^ INT_MIN, INT_MIN + 1)

    m_scr[...] = jnp.full(m_scr.shape, NEG_BIG, F32)
    l_scr[...] = jnp.zeros_like(l_scr)
    acc_scr[...] = jnp.zeros_like(acc_scr)

    def attend(c_blk, ct_blk, key_blk, bias_of_head):
        mask = key_blk >= tau
        for h in range(nh):
            hs = slice(h * QB, (h + 1) * QB)
            s = jnp.dot(c_blk, qlt_ref[0, 0, :, hs], preferred_element_type=F32)
            s = jnp.where(mask, s + bias_of_head(hs), NEG_BIG)
            m_prev = m_scr[h:h + 1, :]
            m_new = jnp.maximum(m_prev, jnp.max(s, axis=0, keepdims=True))
            alpha = jnp.exp(m_prev - m_new)
            p = jnp.exp(s - m_new)
            l_scr[h:h + 1, :] = alpha * l_scr[h:h + 1, :] + jnp.sum(p, axis=0, keepdims=True)
            acc_scr[h] = alpha * acc_scr[h] + jnp.dot(ct_blk, p.astype(BF16),
                                                      preferred_element_type=F32)
            m_scr[h:h + 1, :] = m_new

    far_bias = lambda hs: bfar_ref[:, hs]

    def far_body(kb, carry):
        attend(c_ref[0, kb], ct_ref[0, kb], keys_scr[kb], far_bias)
        return carry

    lax.fori_loop(0, jnp.maximum(j - 1, 0), far_body, 0)

    @pl.when(j >= 1)
    def _():
        attend(c_ref[0, j - 1], ct_ref[0, j - 1], keys_scr[j - 1], lambda hs: bprev_ref[:, hs])
        attend(cm_ref[0, 0], ctm_ref[0, 0], keysm_scr[...], far_bias)

    @pl.when(j == 0)
    def _():
        attend(cm_ref[0, 0], ctm_ref[0, 0], keysm_scr[...], lambda hs: bmeta_ref[:, hs])

    attend(c_ref[0, j], ct_ref[0, j], keys_scr[j], lambda hs: bcur_ref[:, hs])

    outs = []
    for h in range(nh):
        u = (acc_scr[h] * (1.0 / l_scr[h:h + 1, :])).astype(BF16)
        outs.append(jnp.dot(wuvt_ref[h], u, preferred_element_type=F32))
    o_t = jnp.concatenate(outs, axis=0)
    o_n = o_t.T.astype(BF16)
    o_ref[0] = h_ref[0] + jnp.dot(o_n, wout_ref[...], preferred_element_type=F32)


def _dsa_core(h, qlt, qit, wt, kn, c, ct, km, cm, ctm, bcur, bprev, bmeta, bfar, wuvt, wout,
              *, n_sel):
    b, s, d = h.shape
    nkb = s // QB
    head_dim = wuvt.shape[1]
    kern = functools.partial(_dsa_core_kernel, n_sel=n_sel, head_dim=head_dim)
    per_q = lambda r, cdim: pl.BlockSpec((1, 1, r, cdim), lambda i, j: (i, j, 0, 0))
    per_b = lambda r, cdim: pl.BlockSpec((1, nkb, r, cdim), lambda i, j: (i, 0, 0, 0),
                                         pipeline_mode=pl.Buffered(1))
    row = pl.BlockSpec((1, QB, d), lambda i, j: (i, j, 0))
    return pl.pallas_call(
        kern,
        grid=(b, nkb),
        in_specs=[row, per_q(KV_LATENT, ATT_HEADS * QB), per_q(IDX_DIM, IDX_HEADS * QB),
                  per_q(IDX_HEADS, QB),
                  per_b(QB, IDX_DIM), per_b(QB, KV_LATENT), per_b(KV_LATENT, QB)] +
                 [_full(a.shape) for a in (km, cm, ctm, bcur, bprev, bmeta, bfar, wuvt, wout)],
        out_specs=row,
        out_shape=jax.ShapeDtypeStruct((b, s, d), F32),
        scratch_shapes=[pltpu.VMEM((nkb, QB, QB), I32),
                        pltpu.VMEM((META_ROWS, QB), I32),
                        pltpu.VMEM((ATT_HEADS, QB), F32),
                        pltpu.VMEM((ATT_HEADS, QB), F32),
                        pltpu.VMEM((ATT_HEADS, KV_LATENT, QB), F32)],
        compiler_params=_params(2),
        name="dsa_core",
    )(h, qlt, qit, wt, kn, c, ct, km, cm, ctm, bcur, bprev, bmeta, bfar, wuvt, wout)


def _rel_bucket(rel):
    nb = REL_BUCKETS // 2
    max_exact = nb // 2
    ret = jnp.where(rel > 0, nb, 0)
    n = jnp.abs(rel)
    nf = jnp.maximum(n, 1).astype(F32)
    large = max_exact + (jnp.log(nf / max_exact) / math.log(REL_MAX_DIST / max_exact)
                         * (nb - max_exact)).astype(I32)
    large = jnp.minimum(large, nb - 1)
    return ret + jnp.where(n < max_exact, n, large)


def _bias_table(rel_bias, rel):
    t = rel_bias[_rel_bucket(rel)]
    return jnp.transpose(t, (0, 2, 1)).reshape(rel.shape[0], -1)


def kernel(x, meta, norm_mix, norm_ffn, norm_final, gla_w_in, gla_w_a1, gla_w_a2, gla_b_a,
           gla_g_out, gla_w_out, dsa_w_in, dsa_g_kv, dsa_w_uk, dsa_w_uv, dsa_w_out, rel_bias,
           ffn_w_in, ffn_w_out):
    b, s, d = x.shape
    assert s % QB == 0 and d % LANES == 0
    key_dim = d // 2
    head_dim = d // ATT_HEADS
    d_ff = ffn_w_out.shape[1]
    ff_tile = d_ff // 2
    n_sel = min(TOPK_MAX, s // 4)
    row = lambda a: a.reshape(1, -1).astype(F32)
    bf = lambda a: a.astype(BF16)

    meta_x = jnp.zeros((1, META_ROWS, d), F32).at[0, :N_META].set(meta.astype(F32))
    rows = 512 if s % 512 == 0 else QB

    wa1 = jnp.zeros((d, LANES), F32).at[:, :GLA_GATE_RANK].set(gla_w_a1[0])
    wa2 = jnp.zeros((LANES, key_dim), F32).at[:GLA_GATE_RANK].set(gla_w_a2[0])
    gla_args = (row(norm_mix[0]), bf(gla_w_in[0]), bf(wa1), bf(wa2), row(gla_b_a[0]))
    gout = row(gla_g_out[0])
    s0 = jnp.zeros((GLA_HEADS, d // GLA_HEADS, key_dim // GLA_HEADS), F32)
    w_out0 = bf(gla_w_out[0])
    ffn0 = (row(norm_ffn[0]), bf(ffn_w_in[0]), bf(ffn_w_out[0]), row(norm_final))

    mq, mk, mv, mr, mg = _gla_in(meta_x, *gla_args, rows=META_ROWS)
    mo, s_meta = _gla_scan(mq, mk, mv, mr, mg, gout, s0, rows=META_ROWS, n_valid=N_META)
    h_m = _proj_residual(meta_x, mo, w_out0, rows=META_ROWS)
    h_m = _ffn(h_m, *ffn0, rows=META_ROWS, ff_tile=ff_tile, final_norm=False)

    fq, fk, fv, fr, fg = _gla_in(x, *gla_args, rows=rows)
    fo, _ = _gla_scan(fq, fk, fv, fr, fg, gout, s_meta[0], rows=rows, n_valid=None)
    h_f = _proj_residual(x, fo, w_out0, rows=rows)
    h_f = _ffn(h_f, *ffn0, rows=rows, ff_tile=ff_tile, final_norm=False)

    w = dsa_w_in[0]
    s1 = ATT_HEADS * head_dim
    s2 = s1 + KV_LATENT
    s3 = s2 + IDX_HEADS * IDX_DIM
    s4 = s3 + IDX_DIM
    dsa_args = (row(norm_mix[1]), bf(w[:, :s1].T), bf(dsa_w_uk[0]), bf(w[:, s1:s2]),
                bf(w[:, s1:s2].T), row(dsa_g_kv[0]), dsa_g_kv[0].reshape(-1, 1).astype(F32),
                bf(w[:, s2:s3].T), bf(w[:, s3:s4]), bf(w[:, s4:].T))
    _, _, cm, ctm, km, _ = _dsa_in(h_m, *dsa_args, rb=META_ROWS)
    qlt, qit, c, ct, kn, wt = _dsa_in(h_f, *dsa_args, rb=QB)

    ks = jnp.arange(QB, dtype=I32)[:, None]
    qs = jnp.arange(QB, dtype=I32)[None, :]
    bcur = _bias_table(rel_bias, ks - qs)
    bprev = _bias_table(rel_bias, ks - qs - QB)
    bmeta = _bias_table(rel_bias, jnp.arange(META_ROWS, dtype=I32)[:, None] - N_META - qs)
    bfar = _bias_table(rel_bias, jnp.full((1, QB), -(QB + 1), I32))
    wuvt = bf(jnp.transpose(dsa_w_uv[0], (0, 2, 1)))
    h_f = _dsa_core(h_f, qlt, qit, wt, kn, c, ct, km, cm, ctm, bcur, bprev, bmeta, bfar,
                    wuvt, bf(dsa_w_out[0]), n_sel=n_sel)

    ffn1 = (row(norm_ffn[1]), bf(ffn_w_in[1]), bf(ffn_w_out[1]), row(norm_final))
    return _ffn(h_f, *ffn1, rows=rows, ff_tile=ff_tile, final_norm=True).astype(x.dtype)
```

```python
import functools
import math

import jax
import jax.numpy as jnp
from jax import lax
from jax.experimental import pallas as pl
from jax.experimental.pallas import tpu as pltpu

F32 = jnp.float32
BF16 = jnp.bfloat16
I32 = jnp.int32
I16 = jnp.int16

EPS = 1e-6
CHUNK = 64
N_META = 16
GLA_HEADS = 4
GLA_TAU = 16.0
GLA_GATE_RANK = 16
ATT_HEADS = 8
IDX_HEADS = 8
IDX_DIM = 64
KV_LATENT = 256
TOPK_MAX = 256
REL_BUCKETS = 32
REL_MAX_DIST = 128

LANES = 128
SUBLANES = 8
PACKED_ROWS = 16
META_ROWS = 128
QB = 256
VMEM_LIMIT_BYTES = 56 * 1024 * 1024

INT_MIN = -(2 ** 31)
HALF = 2 ** 15
NEG_BIG = -1e30
LOG2_E = math.log2(math.e)

_NT = (((1,), (1,)), ((), ()))
_TN = (((0,), (0,)), ((), ()))


def _params(n_grid):
    return pltpu.CompilerParams(
        dimension_semantics=("arbitrary",) * n_grid,
        vmem_limit_bytes=VMEM_LIMIT_BYTES)


def _full(shape):
    nd = len(shape)
    return pl.BlockSpec(shape, lambda *_: (0,) * nd, pipeline_mode=pl.Buffered(1))


def _rms(x, gain):
    return x * lax.rsqrt(jnp.mean(x * x, axis=-1, keepdims=True) + EPS) * gain


def _silu(x):
    return x / (1.0 + jnp.exp(-x))


def _gla_in_kernel(x_ref, gn_ref, w_ref, wa1_ref, wa2_ref, ba_ref,
                   q_ref, k_ref, v_ref, r_ref, g_ref, *, key_dim, d_model, q_scale):
    hb = _rms(x_ref[0], gn_ref[...]).astype(BF16)
    o = 0
    q = jnp.dot(hb, w_ref[:, o:o + key_dim], preferred_element_type=F32)
    q_ref[0] = (q * q_scale).astype(BF16)
    o += key_dim
    k_ref[0] = jnp.dot(hb, w_ref[:, o:o + key_dim], preferred_element_type=F32).astype(BF16)
    o += key_dim
    v_ref[0] = jnp.dot(hb, w_ref[:, o:o + d_model], preferred_element_type=F32).astype(BF16)
    o += d_model
    r_ref[0] = jnp.dot(hb, w_ref[:, o:o + d_model], preferred_element_type=F32).astype(BF16)
    a = jnp.dot(hb, wa1_ref[...], preferred_element_type=F32).astype(BF16)
    z = jnp.dot(a, wa2_ref[...], preferred_element_type=F32) + ba_ref[...]
    log_sig = jnp.minimum(z, 0.0) - jnp.log(1.0 + jnp.exp(-jnp.abs(z)))
    g_ref[0] = log_sig * (1.0 / GLA_TAU)


def _gla_in(x, gn, w, wa1, wa2, ba, *, rows):
    b, s, d = x.shape
    key_dim = d // 2
    dk = key_dim // GLA_HEADS
    kern = functools.partial(_gla_in_kernel, key_dim=key_dim, d_model=d, q_scale=dk ** -0.5)
    row = lambda width: pl.BlockSpec((1, rows, width), lambda i, j: (i, j, 0))
    return pl.pallas_call(
        kern,
        grid=(b, s // rows),
        in_specs=[row(d), _full(gn.shape), _full(w.shape), _full(wa1.shape),
                  _full(wa2.shape), _full(ba.shape)],
        out_specs=[row(key_dim), row(key_dim), row(d), row(d), row(key_dim)],
        out_shape=[jax.ShapeDtypeStruct((b, s, key_dim), BF16),
                   jax.ShapeDtypeStruct((b, s, key_dim), BF16),
                   jax.ShapeDtypeStruct((b, s, d), BF16),
                   jax.ShapeDtypeStruct((b, s, d), BF16),
                   jax.ShapeDtypeStruct((b, s, key_dim), F32)],
        compiler_params=_params(2),
        name="gla_in",
    )(x, gn, w, wa1, wa2, ba)


def _gla_scan_kernel(q_ref, k_ref, v_ref, r_ref, g_ref, gout_ref, s0_ref,
                     o_ref, sfin_ref, s_scr, *, rows, n_valid, dk, dv):
    j = pl.program_id(1)

    @pl.when(j == 0)
    def _():
        s_scr[...] = s0_ref[...]

    ri = lax.broadcasted_iota(I32, (CHUNK, CHUNK), 0)
    ci = lax.broadcasted_iota(I32, (CHUNK, CHUNK), 1)
    tri = (ri >= ci).astype(F32)
    for c in range(rows // CHUNK):
        rs = slice(c * CHUNK, (c + 1) * CHUNK)
        for h in range(GLA_HEADS):
            ks = slice(h * dk, (h + 1) * dk)
            vs = slice(h * dv, (h + 1) * dv)
            g = g_ref[0, rs, ks]
            if n_valid is not None:
                row = j * rows + c * CHUNK + lax.broadcasted_iota(I32, (CHUNK, dk), 0)
                g = jnp.where(row < n_valid, g, 0.0)
            cum = jnp.dot(tri, g, preferred_element_type=F32, precision=lax.Precision.HIGHEST)
            tot = cum[CHUNK - 1:CHUNK, :]
            kd = (k_ref[0, rs, ks].astype(F32) * jnp.exp(tot - cum)).astype(BF16)
            upd = lax.dot_general(v_ref[0, rs, vs], kd, _TN, preferred_element_type=F32)
            s_new = jnp.exp(tot) * s_scr[h] + upd
            s_scr[h] = s_new
            o = lax.dot_general(q_ref[0, rs, ks], s_new.astype(BF16), _NT,
                                preferred_element_type=F32)
            on = _rms(o, gout_ref[:, vs])
            rr = r_ref[0, rs, vs].astype(F32)
            o_ref[0, rs, vs] = (on * _silu(rr)).astype(BF16)

    @pl.when(j == pl.num_programs(1) - 1)
    def _():
        sfin_ref[0] = s_scr[...]


def _gla_scan(q, k, v, r, g, gout, s0, *, rows, n_valid):
    b, s, d = v.shape
    key_dim = q.shape[-1]
    dk, dv = key_dim // GLA_HEADS, d // GLA_HEADS
    kern = functools.partial(_gla_scan_kernel, rows=rows, n_valid=n_valid, dk=dk, dv=dv)
    row = lambda width: pl.BlockSpec((1, rows, width), lambda i, j: (i, j, 0))
    return pl.pallas_call(
        kern,
        grid=(b, s // rows),
        in_specs=[row(key_dim), row(key_dim), row(d), row(d), row(key_dim),
                  _full(gout.shape), _full(s0.shape)],
        out_specs=[row(d), pl.BlockSpec((1, GLA_HEADS, dv, dk), lambda i, j: (i, 0, 0, 0))],
        out_shape=[jax.ShapeDtypeStruct((b, s, d), BF16),
                   jax.ShapeDtypeStruct((b, GLA_HEADS, dv, dk), F32)],
        scratch_shapes=[pltpu.VMEM((GLA_HEADS, dv, dk), F32)],
        compiler_params=_params(2),
        name="gla_scan",
    )(q, k, v, r, g, gout, s0)


def _proj_residual_kernel(h_ref, a_ref, w_ref, o_ref):
    o_ref[0] = h_ref[0] + jnp.dot(a_ref[0], w_ref[...], preferred_element_type=F32)


def _proj_residual(h, a, w, *, rows):
    b, s, d = h.shape
    row = lambda width: pl.BlockSpec((1, rows, width), lambda i, j: (i, j, 0))
    return pl.pallas_call(
        _proj_residual_kernel,
        grid=(b, s // rows),
        in_specs=[row(d), row(a.shape[-1]), _full(w.shape)],
        out_specs=row(d),
        out_shape=jax.ShapeDtypeStruct((b, s, d), F32),
        compiler_params=_params(2),
        name="proj_residual",
    )(h, a, w)


def _ffn_kernel(h_ref, gn_ref, wg_ref, wu_ref, wo_ref, gf_ref, o_ref, hn_scr, acc_scr,
                *, final_norm):
    t = pl.program_id(2)

    @pl.when(t == 0)
    def _():
        hn_scr[...] = _rms(h_ref[0], gn_ref[...]).astype(BF16)
        acc_scr[...] = jnp.zeros_like(acc_scr)

    hb = hn_scr[...]
    gate = jnp.dot(hb, wg_ref[...], preferred_element_type=F32)
    up = jnp.dot(hb, wu_ref[...], preferred_element_type=F32)
    act = (_silu(gate) * up).astype(BF16)
    acc_scr[...] += jnp.dot(act, wo_ref[...], preferred_element_type=F32)

    @pl.when(t == pl.num_programs(2) - 1)
    def _():
        y = h_ref[0] + acc_scr[...]
        if final_norm:
            y = _rms(y, gf_ref[...])
        o_ref[0] = y


def _ffn(h, gn, w_in, w_out, gf, *, rows, ff_tile, final_norm):
    b, s, d = h.shape
    d_ff = w_out.shape[0]
    n_t = d_ff // ff_tile
    kern = functools.partial(_ffn_kernel, final_norm=final_norm)
    row = pl.BlockSpec((1, rows, d), lambda i, j, t: (i, j, 0))
    return pl.pallas_call(
        kern,
        grid=(b, s // rows, n_t),
        in_specs=[row, _full(gn.shape),
                  pl.BlockSpec((d, ff_tile), lambda i, j, t: (0, t)),
                  pl.BlockSpec((d, ff_tile), lambda i, j, t: (0, n_t + t)),
                  pl.BlockSpec((ff_tile, d), lambda i, j, t: (t, 0)),
                  _full(gf.shape)],
        out_specs=row,
        out_shape=jax.ShapeDtypeStruct((b, s, d), F32),
        scratch_shapes=[pltpu.VMEM((rows, d), BF16), pltpu.VMEM((rows, d), F32)],
        compiler_params=_params(3),
        name="ffn",
    )(h, gn, w_in, w_in, w_out, gf)


def _dsa_in_kernel(h_ref, gn_ref, wqt_ref, wuk_ref, wc_ref, wct_ref, gkv_ref, gkvc_ref,
                   wqit_ref, wk_ref, wwt_ref,
                   qlt_ref, qit_ref, c_ref, ct_ref, k_ref, wt_ref, *, rb, head_dim):
    hb = _rms(h_ref[0], gn_ref[...]).astype(BF16)
    qt = lax.dot_general(wqt_ref[...], hb, _NT, preferred_element_type=F32)
    for h in range(ATT_HEADS):
        qh = qt[h * head_dim:(h + 1) * head_dim, :].astype(BF16)
        ql = jnp.dot(wuk_ref[h], qh, preferred_element_type=F32)
        qlt_ref[0, 0, :, h * rb:(h + 1) * rb] = (ql * (LOG2_E * head_dim ** -0.5)).astype(BF16)
    qit = lax.dot_general(wqit_ref[...], hb, _NT, preferred_element_type=F32)
    qit = (qit * IDX_DIM ** -0.5).astype(BF16)
    for h in range(IDX_HEADS):
        qit_ref[0, 0, :, h * rb:(h + 1) * rb] = qit[h * IDX_DIM:(h + 1) * IDX_DIM, :]
    c = jnp.dot(hb, wc_ref[...], preferred_element_type=F32)
    c_ref[0, 0] = _rms(c, gkv_ref[...]).astype(BF16)
    ct = lax.dot_general(wct_ref[...], hb, _NT, preferred_element_type=F32)
    ct = ct * lax.rsqrt(jnp.mean(ct * ct, axis=0, keepdims=True) + EPS) * gkvc_ref[...]
    ct_ref[0, 0] = ct.astype(BF16)
    k_ref[0, 0] = jnp.dot(hb, wk_ref[...], preferred_element_type=F32).astype(BF16)
    wt = lax.dot_general(wwt_ref[...], hb, _NT, preferred_element_type=F32)
    wt_ref[0, 0] = wt * IDX_HEADS ** -0.5


def _dsa_in(h, gn, wqt, wuk, wc, wct, gkv, gkvc, wqit, wk, wwt, *, rb):
    b, s, d = h.shape
    nb = s // rb
    head_dim = wuk.shape[-1]
    kern = functools.partial(_dsa_in_kernel, rb=rb, head_dim=head_dim)
    blk = lambda r, c: pl.BlockSpec((1, 1, r, c), lambda i, j: (i, j, 0, 0))
    shp = lambda r, c, dt: jax.ShapeDtypeStruct((b, nb, r, c), dt)
    return pl.pallas_call(
        kern,
        grid=(b, nb),
        in_specs=[pl.BlockSpec((1, rb, d), lambda i, j: (i, j, 0))] +
                 [_full(a.shape) for a in (gn, wqt, wuk, wc, wct, gkv, gkvc, wqit, wk, wwt)],
        out_specs=[blk(KV_LATENT, ATT_HEADS * rb), blk(IDX_DIM, IDX_HEADS * rb),
                   blk(rb, KV_LATENT), blk(KV_LATENT, rb), blk(rb, IDX_DIM),
                   blk(IDX_HEADS, rb)],
        out_shape=[shp(KV_LATENT, ATT_HEADS * rb, BF16), shp(IDX_DIM, IDX_HEADS * rb, BF16),
                   shp(rb, KV_LATENT, BF16), shp(KV_LATENT, rb, BF16), shp(rb, IDX_DIM, BF16),
                   shp(IDX_HEADS, rb, F32)],
        compiler_params=_params(2),
        name="dsa_in",
    )(h, gn, wqt, wuk, wc, wct, gkv, gkvc, wqit, wk, wwt)


def _sort_key(score):
    score = jnp.where(score == 0.0, 0.0, score)
    bits = pltpu.bitcast(score, I32)
    return bits ^ ((bits >> 31) & 0x7FFFFFFF)


def _dsa_core_kernel(h_ref, qlt_ref, qit_ref, wt_ref, k_ref, c_ref, ct_ref,
                     km_ref, cm_ref, ctm_ref, bnear_ref, bmeta_ref, bfar_ref,
                     wuvt_ref, wout_ref, o_ref,
                     keys_scr, keysm_scr, k16_scr, k16m_scr, m_scr, l_scr, acc_scr,
                     *, n_sel, head_dim):
    j = pl.program_id(1)
    nh = ATT_HEADS

    def scores(k_blk):
        s8 = jnp.dot(k_blk, qit_ref[0, 0], preferred_element_type=F32)
        acc = None
        for h in range(IDX_HEADS):
            term = jnp.maximum(s8[:, h * QB:(h + 1) * QB], 0.0) * wt_ref[0, 0, h:h + 1, :]
            acc = term if acc is None else acc + term
        return _sort_key(acc)

    def hi16(keys):
        return (keys >> 16).astype(I16)

    def score_pair(i, carry):
        kb = 2 * i
        keys = scores(k_ref[0, pl.ds(kb, 2)].reshape(2 * QB, IDX_DIM))
        keys_scr[pl.ds(kb, 2)] = keys.reshape(2, QB, QB)
        k16_scr[pl.ds(kb, 2)] = hi16(keys).reshape(2, QB, QB)
        return carry

    lax.fori_loop(0, j // 2, score_pair, 0)

    @pl.when(j % 2 == 1)
    def _():
        keys = scores(k_ref[0, j - 1])
        keys_scr[j - 1] = keys
        k16_scr[j - 1] = hi16(keys)

    s_chunk = lax.broadcasted_iota(I32, (QB, QB), 0) // CHUNK
    q_chunk = lax.broadcasted_iota(I32, (QB, QB), 1) // CHUNK
    keys = jnp.where(s_chunk <= q_chunk, scores(k_ref[0, j]), INT_MIN)
    keys_scr[j] = keys
    k16_scr[j] = hi16(keys)
    m_row = lax.broadcasted_iota(I32, (META_ROWS, QB), 0)
    keys = jnp.where(m_row < N_META, scores(km_ref[0, 0]), INT_MIN)
    keysm_scr[...] = keys
    k16m_scr[...] = hi16(keys)

    def count(blocks_ref, meta_ref, rows, dtype, hit):
        zero = jnp.zeros((rows, QB), dtype)
        one = jnp.ones((), dtype)

        def add_rows(load, n_rows, accs):
            accs = list(accs)
            for r in range(n_rows // rows):
                a = accs[r % len(accs)]
                accs[r % len(accs)] = jnp.where(hit(load(r)), a + one, a)
            return tuple(accs)

        def kb_body(kb, accs):
            return add_rows(lambda r: blocks_ref[kb, r * rows:(r + 1) * rows, :], QB, accs)

        accs = lax.fori_loop(0, j + 1, kb_body, (zero,) * 4)
        accs = add_rows(lambda r: meta_ref[r * rows:(r + 1) * rows, :], META_ROWS, accs)
        tot = (accs[0] + accs[1]) + (accs[2] + accs[3])
        return jnp.sum(tot.astype(I32), axis=0, keepdims=True)

    def to16(v):
        return jnp.broadcast_to(v, (PACKED_ROWS, QB)).astype(I16)

    def radix16(need):
        def bit_body(i, res_u):
            cand_u = res_u | jnp.left_shift(jnp.int32(1), 15 - i)
            cand = to16(cand_u - HALF)
            cnt = count(k16_scr, k16m_scr, PACKED_ROWS, I16, lambda e: e >= cand)
            return jnp.where(cnt >= need, cand_u, res_u)

        return lax.fori_loop(0, 16, bit_body, jnp.zeros((1, QB), I32))

    hi_sel = radix16(n_sel) - HALF
    hi_sel16 = to16(hi_sel)
    above = count(k16_scr, k16m_scr, PACKED_ROWS, I16, lambda e: e > hi_sel16)

    def low_half(keys):
        lo = jnp.where((keys >> 16) == hi_sel, (keys & (2 * HALF - 1)) - HALF, -HALF)
        return lo.astype(I16)

    def low_body(kb, carry):
        k16_scr[kb] = low_half(keys_scr[kb])
        return carry

    lax.fori_loop(0, j + 1, low_body, 0)
    k16m_scr[...] = low_half(keysm_scr[...])
    lo_sel = radix16(n_sel - above)
    tau = jnp.maximum(jnp.left_shift(hi_sel, 16) | lo_sel, INT_MIN + 1)

    tau8 = jnp.broadcast_to(tau, (SUBLANES, QB))
    at_least = count(keys_scr, keysm_scr, SUBLANES, I32, lambda e: e >= tau8)

    @pl.when(jnp.max(at_least) > n_sel)
    def _():
        quota = n_sel - count(keys_scr, keysm_scr, SUBLANES, I32, lambda e: e > tau8)

        def demote(ref, n, seen):
            keys = ref[...]
            tie = keys == tau
            tri = (lax.broadcasted_iota(I32, (n, n), 0) >=
                   lax.broadcasted_iota(I32, (n, n), 1)).astype(BF16)
            rank = jnp.dot(tri, jnp.where(tie, 1.0, 0.0).astype(BF16),
                           preferred_element_type=F32).astype(I32)
            ref[...] = jnp.where(tie & (seen + rank > quota), tau - 1, keys)
            return seen + rank[n - 1:n, :]

        seen = demote(keysm_scr, META_ROWS, jnp.zeros((1, QB), I32))
        lax.fori_loop(0, j + 1, lambda kb, seen: demote(keys_scr.at[kb], QB, seen), seen)

    m_scr[...] = jnp.full(m_scr.shape, NEG_BIG, F32)
    l_scr[...] = jnp.zeros_like(l_scr)
    acc_scr[...] = jnp.zeros_like(acc_scr)

    def attend(c_blk, ct_blk, key_blk, bias_tab=None, bias_row=None):
        cap = jnp.where(key_blk >= tau, -NEG_BIG, NEG_BIG)
        for h in range(nh):
            hs = slice(h * QB, (h + 1) * QB)
            s = jnp.dot(c_blk, qlt_ref[0, 0, :, hs], preferred_element_type=F32)
            if bias_tab is not None:
                s = s + bias_tab[:, hs]
            s = jnp.minimum(s, cap)
            blk_max = jnp.max(s, axis=0, keepdims=True)
            m_prev = m_scr[h:h + 1, :]
            if bias_row is not None:
                b = bias_row[:, hs]
                m_new = jnp.maximum(m_prev, blk_max + b)
                shift = m_new - b
            else:
                m_new = jnp.maximum(m_prev, blk_max)
                shift = m_new
            alpha = jnp.exp2(m_prev - m_new)
            p = jnp.exp2(s - shift)
            l_scr[h:h + 1, :] = alpha * l_scr[h:h + 1, :] + jnp.sum(p, axis=0, keepdims=True)
            acc_scr[h] = alpha * acc_scr[h] + jnp.dot(ct_blk, p.astype(BF16),
                                                      preferred_element_type=F32)
            m_scr[h:h + 1, :] = m_new

    def attend_pair(kb, **bias):
        attend(c_ref[0, pl.ds(kb, 2)].reshape(2 * QB, KV_LATENT),
               jnp.concatenate([ct_ref[0, kb], ct_ref[0, kb + 1]], axis=1),
               keys_scr[pl.ds(kb, 2)].reshape(2 * QB, QB), **bias)

    n_far = jnp.maximum(j - 1, 0)

    def far_body(i, carry):
        attend_pair(2 * i, bias_row=bfar_ref)
        return carry

    lax.fori_loop(0, n_far // 2, far_body, 0)

    @pl.when(n_far % 2 == 1)
    def _():
        attend(c_ref[0, n_far - 1], ct_ref[0, n_far - 1], keys_scr[n_far - 1], bias_row=bfar_ref)

    @pl.when(j >= 1)
    def _():
        attend_pair(j - 1, bias_tab=bnear_ref)
        attend(cm_ref[0, 0], ctm_ref[0, 0], keysm_scr[...], bias_row=bfar_ref)

    @pl.when(j == 0)
    def _():
        attend(c_ref[0, 0], ct_ref[0, 0], keys_scr[0], bias_tab=bnear_ref.at[QB:, :])
        attend(cm_ref[0, 0], ctm_ref[0, 0], keysm_scr[...], bias_tab=bmeta_ref)

    outs = []
    for h in range(nh):
        u = (acc_scr[h] * (1.0 / l_scr[h:h + 1, :])).astype(BF16)
        outs.append(jnp.dot(wuvt_ref[h], u, preferred_element_type=F32))
    o_t = jnp.concatenate(outs, axis=0)
    o_n = o_t.T.astype(BF16)
    o_ref[0] = h_ref[0] + jnp.dot(o_n, wout_ref[...], preferred_element_type=F32)


def _dsa_core(h, qlt, qit, wt, kn, c, ct, km, cm, ctm, bnear, bmeta, bfar, wuvt, wout,
              *, n_sel):
    b, s, d = h.shape
    nkb = s // QB
    head_dim = wuvt.shape[1]
    kern = functools.partial(_dsa_core_kernel, n_sel=n_sel, head_dim=head_dim)
    per_q = lambda r, cdim: pl.BlockSpec((1, 1, r, cdim), lambda i, j: (i, j, 0, 0))
    per_b = lambda r, cdim: pl.BlockSpec((1, nkb, r, cdim), lambda i, j: (i, 0, 0, 0),
                                         pipeline_mode=pl.Buffered(1))
    row = pl.BlockSpec((1, QB, d), lambda i, j: (i, j, 0))
    return pl.pallas_call(
        kern,
        grid=(b, nkb),
        in_specs=[row, per_q(KV_LATENT, ATT_HEADS * QB), per_q(IDX_DIM, IDX_HEADS * QB),
                  per_q(IDX_HEADS, QB),
                  per_b(QB, IDX_DIM), per_b(QB, KV_LATENT), per_b(KV_LATENT, QB)] +
                 [_full(a.shape) for a in (km, cm, ctm, bnear, bmeta, bfar, wuvt, wout)],
        out_specs=row,
        out_shape=jax.ShapeDtypeStruct((b, s, d), F32),
        scratch_shapes=[pltpu.VMEM((nkb, QB, QB), I32),
                        pltpu.VMEM((META_ROWS, QB), I32),
                        pltpu.VMEM((nkb, QB, QB), I16),
                        pltpu.VMEM((META_ROWS, QB), I16),
                        pltpu.VMEM((ATT_HEADS, QB), F32),
                        pltpu.VMEM((ATT_HEADS, QB), F32),
                        pltpu.VMEM((ATT_HEADS, KV_LATENT, QB), F32)],
        compiler_params=_params(2),
        name="dsa_core",
    )(h, qlt, qit, wt, kn, c, ct, km, cm, ctm, bnear, bmeta, bfar, wuvt, wout)


def _rel_bucket(rel):
    nb = REL_BUCKETS // 2
    max_exact = nb // 2
    ret = jnp.where(rel > 0, nb, 0)
    n = jnp.abs(rel)
    nf = jnp.maximum(n, 1).astype(F32)
    large = max_exact + (jnp.log(nf / max_exact) / math.log(REL_MAX_DIST / max_exact)
                         * (nb - max_exact)).astype(I32)
    large = jnp.minimum(large, nb - 1)
    return ret + jnp.where(n < max_exact, n, large)


def _bias_table(rel_bias, rel):
    onehot = (_rel_bucket(rel)[:, :, None] == jnp.arange(REL_BUCKETS, dtype=I32)).astype(F32)
    t = jnp.einsum("kqb,bh->khq", onehot, rel_bias.astype(F32) * LOG2_E,
                   precision=lax.Precision.HIGHEST)
    return t.reshape(rel.shape[0], -1)


def kernel(x, meta, norm_mix, norm_ffn, norm_final, gla_w_in, gla_w_a1, gla_w_a2, gla_b_a,
           gla_g_out, gla_w_out, dsa_w_in, dsa_g_kv, dsa_w_uk, dsa_w_uv, dsa_w_out, rel_bias,
           ffn_w_in, ffn_w_out):
    b, s, d = x.shape
    assert s % QB == 0 and d % LANES == 0
    key_dim = d // 2
    head_dim = d // ATT_HEADS
    d_ff = ffn_w_out.shape[1]
    ff_tile = d_ff // 2
    n_sel = min(TOPK_MAX, s // 4)
    row = lambda a: a.reshape(1, -1).astype(F32)
    bf = lambda a: a.astype(BF16)

    meta_x = jnp.zeros((1, META_ROWS, d), F32).at[0, :N_META].set(meta.astype(F32))
    rows = 512 if s % 512 == 0 else QB

    wa1 = jnp.zeros((d, LANES), F32).at[:, :GLA_GATE_RANK].set(gla_w_a1[0])
    wa2 = jnp.zeros((LANES, key_dim), F32).at[:GLA_GATE_RANK].set(gla_w_a2[0])
    gla_args = (row(norm_mix[0]), bf(gla_w_in[0]), bf(wa1), bf(wa2), row(gla_b_a[0]))
    gout = row(gla_g_out[0])
    s0 = jnp.zeros((GLA_HEADS, d // GLA_HEADS, key_dim // GLA_HEADS), F32)
    w_out0 = bf(gla_w_out[0])
    ffn0 = (row(norm_ffn[0]), bf(ffn_w_in[0]), bf(ffn_w_out[0]), row(norm_final))

    mq, mk, mv, mr, mg = _gla_in(meta_x, *gla_args, rows=META_ROWS)
    mo, s_meta = _gla_scan(mq, mk, mv, mr, mg, gout, s0, rows=META_ROWS, n_valid=N_META)
    h_m = _proj_residual(meta_x, mo, w_out0, rows=META_ROWS)
    h_m = _ffn(h_m, *ffn0, rows=META_ROWS, ff_tile=ff_tile, final_norm=False)

    fq, fk, fv, fr, fg = _gla_in(x, *gla_args, rows=rows)
    fo, _ = _gla_scan(fq, fk, fv, fr, fg, gout, s_meta[0], rows=rows, n_valid=None)
    h_f = _proj_residual(x, fo, w_out0, rows=rows)
    h_f = _ffn(h_f, *ffn0, rows=rows, ff_tile=ff_tile, final_norm=False)

    w = dsa_w_in[0]
    s1 = ATT_HEADS * head_dim
    s2 = s1 + KV_LATENT
    s3 = s2 + IDX_HEADS * IDX_DIM
    s4 = s3 + IDX_DIM
    dsa_args = (row(norm_mix[1]), bf(w[:, :s1].T), bf(dsa_w_uk[0]), bf(w[:, s1:s2]),
                bf(w[:, s1:s2].T), row(dsa_g_kv[0]), dsa_g_kv[0].reshape(-1, 1).astype(F32),
                bf(w[:, s2:s3].T), bf(w[:, s3:s4]), bf(w[:, s4:].T))
    _, _, cm, ctm, km, _ = _dsa_in(h_m, *dsa_args, rb=META_ROWS)
    qlt, qit, c, ct, kn, wt = _dsa_in(h_f, *dsa_args, rb=QB)

    ks = jnp.arange(2 * QB, dtype=I32)[:, None] - QB
    qs = jnp.arange(QB, dtype=I32)[None, :]
    bnear = _bias_table(rel_bias, ks - qs)
    bmeta = _bias_table(rel_bias, jnp.arange(META_ROWS, dtype=I32)[:, None] - N_META - qs)
    bfar = _bias_table(rel_bias, jnp.full((1, QB), -(QB + 1), I32))
    wuvt = bf(jnp.transpose(dsa_w_uv[0], (0, 2, 1)))
    h_f = _dsa_core(h_f, qlt, qit, wt, kn, c, ct, km, cm, ctm, bnear, bmeta, bfar,
                    wuvt, bf(dsa_w_out[0]), n_sel=n_sel)

    ffn1 = (row(norm_ffn[1]), bf(ffn_w_in[1]), bf(ffn_w_out[1]), row(norm_final))
    return _ffn(h_f, *ffn1, rows=rows, ff_tile=ff_tile, final_norm=True).astype(x.dtype)
```

```python
import functools
import math

import jax
import jax.numpy as jnp
from jax import lax
from jax.experimental import pallas as pl
from jax.experimental.pallas import tpu as pltpu

F32 = jnp.float32
BF16 = jnp.bfloat16
I32 = jnp.int32
I16 = jnp.int16

EPS = 1e-6
CHUNK = 64
N_META = 16
GLA_HEADS = 4
GLA_TAU = 16.0
GLA_GATE_RANK = 16
ATT_HEADS = 8
IDX_HEADS = 8
IDX_DIM = 64
KV_LATENT = 256
TOPK_MAX = 256
REL_BUCKETS = 32
REL_MAX_DIST = 128

LANES = 128
SUBLANES = 8
PACKED_ROWS = 16
META_ROWS = 128
QB = 256
VMEM_LIMIT_BYTES = 56 * 1024 * 1024

INT_MIN = -(2 ** 31)
HALF = 2 ** 15
NEG_BIG = -1e30
LOG2_E = math.log2(math.e)

_NT = (((1,), (1,)), ((), ()))
_TN = (((0,), (0,)), ((), ()))


def _params(n_grid):
    return pltpu.CompilerParams(
        dimension_semantics=("arbitrary",) * n_grid,
        vmem_limit_bytes=VMEM_LIMIT_BYTES)


def _full(shape):
    nd = len(shape)
    return pl.BlockSpec(shape, lambda *_: (0,) * nd, pipeline_mode=pl.Buffered(1))


def _rms(x, gain):
    return x * lax.rsqrt(jnp.mean(x * x, axis=-1, keepdims=True) + EPS) * gain


def _silu(x):
    return x / (1.0 + jnp.exp(-x))


def _gla_in_kernel(x_ref, gn_ref, w_ref, wa1_ref, wa2_ref, ba_ref,
                   q_ref, kd_ref, v_ref, r_ref, dtot_ref,
                   *, rows, key_dim, d_model, q_scale, n_valid):
    hb = _rms(x_ref[0], gn_ref[...]).astype(BF16)
    o = 0
    q = jnp.dot(hb, w_ref[:, o:o + key_dim], preferred_element_type=F32)
    q_ref[0] = (q * q_scale).astype(BF16)
    o += key_dim
    k = jnp.dot(hb, w_ref[:, o:o + key_dim], preferred_element_type=F32)
    o += key_dim
    v_ref[0] = jnp.dot(hb, w_ref[:, o:o + d_model], preferred_element_type=F32).astype(BF16)
    o += d_model
    r_ref[0] = jnp.dot(hb, w_ref[:, o:o + d_model], preferred_element_type=F32).astype(BF16)
    a = jnp.dot(hb, wa1_ref[...], preferred_element_type=F32).astype(BF16)
    z = jnp.dot(a, wa2_ref[...], preferred_element_type=F32) + ba_ref[...]
    log_sig = jnp.minimum(z, 0.0) - jnp.log(1.0 + jnp.exp(-jnp.abs(z)))
    cum = log_sig * (1.0 / GLA_TAU)
    row = lax.broadcasted_iota(I32, (rows, key_dim), 0)
    if n_valid is not None:
        cum = jnp.where(pl.program_id(1) * rows + row < n_valid, cum, 0.0)
    pos = row % CHUNK
    step = 1
    while step < CHUNK:
        cum = cum + jnp.where(pos >= step, pltpu.roll(cum, step, axis=0), 0.0)
        step *= 2
    n_chunks = rows // CHUNK
    cum = cum.reshape(n_chunks, CHUNK, key_dim)
    tot = cum[:, CHUNK - 1:CHUNK, :]
    kd = k.reshape(n_chunks, CHUNK, key_dim) * jnp.exp(tot - cum)
    kd_ref[0] = kd.reshape(rows, key_dim).astype(BF16)
    dtot_ref[0] = jnp.exp(tot).reshape(n_chunks, key_dim)


def _gla_in(x, gn, w, wa1, wa2, ba, *, rows, n_valid):
    b, s, d = x.shape
    key_dim = d // 2
    dk = key_dim // GLA_HEADS
    kern = functools.partial(_gla_in_kernel, rows=rows, key_dim=key_dim, d_model=d,
                             q_scale=dk ** -0.5, n_valid=n_valid)
    row = lambda width: pl.BlockSpec((1, rows, width), lambda i, j: (i, j, 0))
    return pl.pallas_call(
        kern,
        grid=(b, s // rows),
        in_specs=[row(d), _full(gn.shape), _full(w.shape), _full(wa1.shape),
                  _full(wa2.shape), _full(ba.shape)],
        out_specs=[row(key_dim), row(key_dim), row(d), row(d),
                   pl.BlockSpec((1, rows // CHUNK, key_dim), lambda i, j: (i, j, 0))],
        out_shape=[jax.ShapeDtypeStruct((b, s, key_dim), BF16),
                   jax.ShapeDtypeStruct((b, s, key_dim), BF16),
                   jax.ShapeDtypeStruct((b, s, d), BF16),
                   jax.ShapeDtypeStruct((b, s, d), BF16),
                   jax.ShapeDtypeStruct((b, s // CHUNK, key_dim), F32)],
        compiler_params=_params(2),
        name="gla_in",
    )(x, gn, w, wa1, wa2, ba)


def _gla_scan_kernel(q_ref, kd_ref, v_ref, r_ref, dtot_ref, gout_ref, s0_ref,
                     o_ref, sfin_ref, s_scr, *, rows, dk, dv):
    j = pl.program_id(1)

    @pl.when(j == 0)
    def _():
        s_scr[...] = s0_ref[...]

    for c in range(rows // CHUNK):
        rs = slice(c * CHUNK, (c + 1) * CHUNK)
        for h in range(GLA_HEADS):
            ks = slice(h * dk, (h + 1) * dk)
            vs = slice(h * dv, (h + 1) * dv)
            upd = lax.dot_general(v_ref[0, rs, vs], kd_ref[0, rs, ks], _TN,
                                  preferred_element_type=F32)
            s_new = dtot_ref[0, c:c + 1, ks] * s_scr[h] + upd
            s_scr[h] = s_new
            o = lax.dot_general(q_ref[0, rs, ks], s_new.astype(BF16), _NT,
                                preferred_element_type=F32)
            on = _rms(o, gout_ref[:, vs])
            rr = r_ref[0, rs, vs].astype(F32)
            o_ref[0, rs, vs] = (on * _silu(rr)).astype(BF16)

    @pl.when(j == pl.num_programs(1) - 1)
    def _():
        sfin_ref[0] = s_scr[...]


def _gla_scan(q, kd, v, r, dtot, gout, s0, *, rows):
    b, s, d = v.shape
    key_dim = q.shape[-1]
    dk, dv = key_dim // GLA_HEADS, d // GLA_HEADS
    kern = functools.partial(_gla_scan_kernel, rows=rows, dk=dk, dv=dv)
    row = lambda width: pl.BlockSpec((1, rows, width), lambda i, j: (i, j, 0))
    return pl.pallas_call(
        kern,
        grid=(b, s // rows),
        in_specs=[row(key_dim), row(key_dim), row(d), row(d),
                  pl.BlockSpec((1, rows // CHUNK, key_dim), lambda i, j: (i, j, 0)),
                  _full(gout.shape), _full(s0.shape)],
        out_specs=[row(d), pl.BlockSpec((1, GLA_HEADS, dv, dk), lambda i, j: (i, 0, 0, 0))],
        out_shape=[jax.ShapeDtypeStruct((b, s, d), BF16),
                   jax.ShapeDtypeStruct((b, GLA_HEADS, dv, dk), F32)],
        scratch_shapes=[pltpu.VMEM((GLA_HEADS, dv, dk), F32)],
        compiler_params=_params(2),
        name="gla_scan",
    )(q, kd, v, r, dtot, gout, s0)


def _ffn_kernel(h_ref, a_ref, wp_ref, gn_ref, wg_ref, wu_ref, wo_ref, gf_ref, o_ref,
                res_scr, hn_scr, acc_scr, *, final_norm):
    t = pl.program_id(2)

    @pl.when(t == 0)
    def _():
        res = h_ref[0] + jnp.dot(a_ref[0], wp_ref[...], preferred_element_type=F32)
        res_scr[...] = res
        hn_scr[...] = _rms(res, gn_ref[...]).astype(BF16)
        acc_scr[...] = jnp.zeros_like(acc_scr)

    hb = hn_scr[...]
    gate = jnp.dot(hb, wg_ref[...], preferred_element_type=F32)
    up = jnp.dot(hb, wu_ref[...], preferred_element_type=F32)
    act = (_silu(gate) * up).astype(BF16)
    acc_scr[...] += jnp.dot(act, wo_ref[...], preferred_element_type=F32)

    @pl.when(t == pl.num_programs(2) - 1)
    def _():
        y = res_scr[...] + acc_scr[...]
        if final_norm:
            y = _rms(y, gf_ref[...])
        o_ref[0] = y


def _ffn(h, a, w_proj, gn, w_in, w_out, gf, *, rows, ff_tile, final_norm):
    b, s, d = h.shape
    d_ff = w_out.shape[0]
    n_t = d_ff // ff_tile
    kern = functools.partial(_ffn_kernel, final_norm=final_norm)
    row = pl.BlockSpec((1, rows, d), lambda i, j, t: (i, j, 0))
    return pl.pallas_call(
        kern,
        grid=(b, s // rows, n_t),
        in_specs=[row, row, _full(w_proj.shape), _full(gn.shape),
                  pl.BlockSpec((d, ff_tile), lambda i, j, t: (0, t)),
                  pl.BlockSpec((d, ff_tile), lambda i, j, t: (0, n_t + t)),
                  pl.BlockSpec((ff_tile, d), lambda i, j, t: (t, 0)),
                  _full(gf.shape)],
        out_specs=row,
        out_shape=jax.ShapeDtypeStruct((b, s, d), F32),
        scratch_shapes=[pltpu.VMEM((rows, d), F32), pltpu.VMEM((rows, d), BF16),
                        pltpu.VMEM((rows, d), F32)],
        compiler_params=_params(3),
        name="ffn",
    )(h, a, w_proj, gn, w_in, w_in, w_out, gf)


def _dsa_in_kernel(h_ref, gn_ref, wqt_ref, wuk_ref, wc_ref, wct_ref, gkv_ref, gkvc_ref,
                   wqit_ref, wk_ref, wwt_ref,
                   qlt_ref, qit_ref, c_ref, ct_ref, k_ref, wt_ref, *, rb, head_dim):
    hb = _rms(h_ref[0], gn_ref[...]).astype(BF16)
    qt = lax.dot_general(wqt_ref[...], hb, _NT, preferred_element_type=F32)
    for h in range(ATT_HEADS):
        qh = qt[h * head_dim:(h + 1) * head_dim, :].astype(BF16)
        ql = jnp.dot(wuk_ref[h], qh, preferred_element_type=F32)
        qlt_ref[0, 0, :, h * rb:(h + 1) * rb] = (ql * (LOG2_E * head_dim ** -0.5)).astype(BF16)
    qit = lax.dot_general(wqit_ref[...], hb, _NT, preferred_element_type=F32)
    qit = (qit * IDX_DIM ** -0.5).astype(BF16)
    for h in range(IDX_HEADS):
        qit_ref[0, 0, :, h * rb:(h + 1) * rb] = qit[h * IDX_DIM:(h + 1) * IDX_DIM, :]
    c = jnp.dot(hb, wc_ref[...], preferred_element_type=F32)
    c_ref[0, 0] = _rms(c, gkv_ref[...]).astype(BF16)
    ct = lax.dot_general(wct_ref[...], hb, _NT, preferred_element_type=F32)
    ct = ct * lax.rsqrt(jnp.mean(ct * ct, axis=0, keepdims=True) + EPS) * gkvc_ref[...]
    ct_ref[0, 0] = ct.astype(BF16)
    k_ref[0, 0] = jnp.dot(hb, wk_ref[...], preferred_element_type=F32).astype(BF16)
    wt = lax.dot_general(wwt_ref[...], hb, _NT, preferred_element_type=F32)
    wt_ref[0, 0] = wt * IDX_HEADS ** -0.5


def _dsa_in(h, gn, wqt, wuk, wc, wct, gkv, gkvc, wqit, wk, wwt, *, rb):
    b, s, d = h.shape
    nb = s // rb
    head_dim = wuk.shape[-1]
    kern = functools.partial(_dsa_in_kernel, rb=rb, head_dim=head_dim)
    blk = lambda r, c: pl.BlockSpec((1, 1, r, c), lambda i, j: (i, j, 0, 0))
    shp = lambda r, c, dt: jax.ShapeDtypeStruct((b, nb, r, c), dt)
    return pl.pallas_call(
        kern,
        grid=(b, nb),
        in_specs=[pl.BlockSpec((1, rb, d), lambda i, j: (i, j, 0))] +
                 [_full(a.shape) for a in (gn, wqt, wuk, wc, wct, gkv, gkvc, wqit, wk, wwt)],
        out_specs=[blk(KV_LATENT, ATT_HEADS * rb), blk(IDX_DIM, IDX_HEADS * rb),
                   blk(rb, KV_LATENT), blk(KV_LATENT, rb), blk(rb, IDX_DIM),
                   blk(IDX_HEADS, rb)],
        out_shape=[shp(KV_LATENT, ATT_HEADS * rb, BF16), shp(IDX_DIM, IDX_HEADS * rb, BF16),
                   shp(rb, KV_LATENT, BF16), shp(KV_LATENT, rb, BF16), shp(rb, IDX_DIM, BF16),
                   shp(IDX_HEADS, rb, F32)],
        compiler_params=_params(2),
        name="dsa_in",
    )(h, gn, wqt, wuk, wc, wct, gkv, gkvc, wqit, wk, wwt)


def _sort_key(score):
    score = jnp.where(score == 0.0, 0.0, score)
    bits = pltpu.bitcast(score, I32)
    return bits ^ ((bits >> 31) & 0x7FFFFFFF)


def _dsa_core_kernel(qlt_ref, qit_ref, wt_ref, k_ref, c_ref, ct_ref,
                     km_ref, cm_ref, ctm_ref, bnear_ref, bmeta_ref, bfar_ref,
                     wuvt_ref, o_ref,
                     keys_scr, keysm_scr, k16_scr, k16m_scr, m_scr, l_scr, acc_scr,
                     *, n_sel, head_dim):
    j = pl.program_id(1)
    nh = ATT_HEADS

    def scores(k_blk):
        s8 = jnp.dot(k_blk, qit_ref[0, 0], preferred_element_type=F32)
        acc = None
        for h in range(IDX_HEADS):
            term = jnp.maximum(s8[:, h * QB:(h + 1) * QB], 0.0) * wt_ref[0, 0, h:h + 1, :]
            acc = term if acc is None else acc + term
        return _sort_key(acc)

    def hi16(keys):
        return (keys >> 16).astype(I16)

    def score_pair(i, carry):
        kb = 2 * i
        keys = scores(k_ref[0, pl.ds(kb, 2)].reshape(2 * QB, IDX_DIM))
        keys_scr[pl.ds(kb, 2)] = keys.reshape(2, QB, QB)
        k16_scr[pl.ds(kb, 2)] = hi16(keys).reshape(2, QB, QB)
        return carry

    lax.fori_loop(0, j // 2, score_pair, 0)

    @pl.when(j % 2 == 1)
    def _():
        keys = scores(k_ref[0, j - 1])
        keys_scr[j - 1] = keys
        k16_scr[j - 1] = hi16(keys)

    s_chunk = lax.broadcasted_iota(I32, (QB, QB), 0) // CHUNK
    q_chunk = lax.broadcasted_iota(I32, (QB, QB), 1) // CHUNK
    keys = jnp.where(s_chunk <= q_chunk, scores(k_ref[0, j]), INT_MIN)
    keys_scr[j] = keys
    k16_scr[j] = hi16(keys)
    m_row = lax.broadcasted_iota(I32, (META_ROWS, QB), 0)
    keys = jnp.where(m_row < N_META, scores(km_ref[0, 0]), INT_MIN)
    keysm_scr[...] = keys
    k16m_scr[...] = hi16(keys)

    def count(blocks_ref, meta_ref, rows, dtype, hit):
        zero = jnp.zeros((rows, QB), dtype)
        one = jnp.ones((), dtype)

        def add_rows(load, n_rows, accs):
            accs = list(accs)
            for r in range(n_rows // rows):
                a = accs[r % len(accs)]
                accs[r % len(accs)] = jnp.where(hit(load(r)), a + one, a)
            return tuple(accs)

        def kb_body(kb, accs):
            return add_rows(lambda r: blocks_ref[kb, r * rows:(r + 1) * rows, :], QB, accs)

        accs = lax.fori_loop(0, j + 1, kb_body, (zero,) * 4)
        accs = add_rows(lambda r: meta_ref[r * rows:(r + 1) * rows, :], META_ROWS, accs)
        tot = (accs[0] + accs[1]) + (accs[2] + accs[3])
        return jnp.sum(tot.astype(I32), axis=0, keepdims=True)

    def to16(v):
        return jnp.broadcast_to(v, (PACKED_ROWS, QB)).astype(I16)

    def radix16(need):
        def bit_body(i, res_u):
            cand_u = res_u | jnp.left_shift(jnp.int32(1), 15 - i)
            cand = to16(cand_u - HALF)
            cnt = count(k16_scr, k16m_scr, PACKED_ROWS, I16, lambda e: e >= cand)
            return jnp.where(cnt >= need, cand_u, res_u)

        return lax.fori_loop(0, 16, bit_body, jnp.zeros((1, QB), I32))

    hi_sel = radix16(n_sel) - HALF
    hi_sel16 = to16(hi_sel)
    above = count(k16_scr, k16m_scr, PACKED_ROWS, I16, lambda e: e > hi_sel16)

    def low_half(keys):
        lo = jnp.where((keys >> 16) == hi_sel, (keys & (2 * HALF - 1)) - HALF, -HALF)
        return lo.astype(I16)

    def low_body(kb, carry):
        k16_scr[kb] = low_half(keys_scr[kb])
        return carry

    lax.fori_loop(0, j + 1, low_body, 0)
    k16m_scr[...] = low_half(keysm_scr[...])
    lo_sel = radix16(n_sel - above)
    tau = jnp.maximum(jnp.left_shift(hi_sel, 16) | lo_sel, INT_MIN + 1)

    tau8 = jnp.broadcast_to(tau, (SUBLANES, QB))
    at_least = count(keys_scr, keysm_scr, SUBLANES, I32, lambda e: e >= tau8)

    @pl.when(jnp.max(at_least) > n_sel)
    def _():
        quota = n_sel - count(keys_scr, keysm_scr, SUBLANES, I32, lambda e: e > tau8)

        def demote(ref, n, seen):
            keys = ref[...]
            tie = keys == tau
            tri = (lax.broadcasted_iota(I32, (n, n), 0) >=
                   lax.broadcasted_iota(I32, (n, n), 1)).astype(BF16)
            rank = jnp.dot(tri, jnp.where(tie, 1.0, 0.0).astype(BF16),
                           preferred_element_type=F32).astype(I32)
            ref[...] = jnp.where(tie & (seen + rank > quota), tau - 1, keys)
            return seen + rank[n - 1:n, :]

        seen = demote(keysm_scr, META_ROWS, jnp.zeros((1, QB), I32))
        lax.fori_loop(0, j + 1, lambda kb, seen: demote(keys_scr.at[kb], QB, seen), seen)

    m_scr[...] = jnp.full(m_scr.shape, NEG_BIG, F32)
    l_scr[...] = jnp.zeros_like(l_scr)
    acc_scr[...] = jnp.zeros_like(acc_scr)

    def cap_of(keys):
        return jnp.where(keys >= tau, -NEG_BIG, NEG_BIG)

    def attend(c_blk, ct_blk, cap, bias_tab=None, bias_row=None):
        for h in range(nh):
            hs = slice(h * QB, (h + 1) * QB)
            s = jnp.dot(c_blk, qlt_ref[0, 0, :, hs], preferred_element_type=F32)
            if bias_tab is not None:
                s = s + bias_tab[:, hs]
            s = jnp.minimum(s, cap)
            blk_max = jnp.max(s, axis=0, keepdims=True)
            m_prev = m_scr[h:h + 1, :]
            if bias_row is not None:
                b = bias_row[:, hs]
                m_new = jnp.maximum(m_prev, blk_max + b)
                shift = m_new - b
            else:
                m_new = jnp.maximum(m_prev, blk_max)
                shift = m_new
            alpha = jnp.exp2(m_prev - m_new)
            p = jnp.exp2(s - shift)
            l_scr[h:h + 1, :] = alpha * l_scr[h:h + 1, :] + jnp.sum(p, axis=0, keepdims=True)
            acc_scr[h] = alpha * acc_scr[h] + jnp.dot(ct_blk, p.astype(BF16),
                                                      preferred_element_type=F32)
            m_scr[h:h + 1, :] = m_new

    def attend_pair(kb, **bias):
        attend(c_ref[0, pl.ds(kb, 2)].reshape(2 * QB, KV_LATENT),
               jnp.concatenate([ct_ref[0, kb], ct_ref[0, kb + 1]], axis=1),
               cap_of(keys_scr[pl.ds(kb, 2)].reshape(2 * QB, QB)), **bias)

    n_far = jnp.maximum(j - 1, 0)
    n_pairs = n_far // 2

    def far_body(i, carry):
        attend_pair(4 * i, bias_row=bfar_ref)
        attend_pair(4 * i + 2, bias_row=bfar_ref)
        return carry

    lax.fori_loop(0, n_pairs // 2, far_body, 0)

    @pl.when(n_pairs % 2 == 1)
    def _():
        attend_pair(2 * (n_pairs - 1), bias_row=bfar_ref)

    @pl.when(n_far % 2 == 1)
    def _():
        attend(c_ref[0, n_far - 1], ct_ref[0, n_far - 1], cap_of(keys_scr[n_far - 1]),
               bias_row=bfar_ref)

    @pl.when(j >= 1)
    def _():
        attend_pair(j - 1, bias_tab=bnear_ref)
        attend(cm_ref[0, 0], ctm_ref[0, 0], cap_of(keysm_scr[...]), bias_row=bfar_ref)

    @pl.when(j == 0)
    def _():
        attend(c_ref[0, 0], ct_ref[0, 0], cap_of(keys_scr[0]), bias_tab=bnear_ref.at[QB:, :])
        attend(cm_ref[0, 0], ctm_ref[0, 0], cap_of(keysm_scr[...]), bias_tab=bmeta_ref)

    outs = []
    for h in range(nh):
        u = (acc_scr[h] * (1.0 / l_scr[h:h + 1, :])).astype(BF16)
        outs.append(jnp.dot(wuvt_ref[h], u, preferred_element_type=F32))
    o_t = jnp.concatenate(outs, axis=0)
    o_ref[0] = o_t.T.astype(BF16)


def _dsa_core(qlt, qit, wt, kn, c, ct, km, cm, ctm, bnear, bmeta, bfar, wuvt, *, n_sel):
    b, nkb = c.shape[:2]
    d = wuvt.shape[0] * wuvt.shape[1]
    head_dim = wuvt.shape[1]
    kern = functools.partial(_dsa_core_kernel, n_sel=n_sel, head_dim=head_dim)
    per_q = lambda r, cdim: pl.BlockSpec((1, 1, r, cdim), lambda i, j: (i, j, 0, 0))
    per_b = lambda r, cdim: pl.BlockSpec((1, nkb, r, cdim), lambda i, j: (i, 0, 0, 0),
                                         pipeline_mode=pl.Buffered(1))
    row = pl.BlockSpec((1, QB, d), lambda i, j: (i, j, 0))
    return pl.pallas_call(
        kern,
        grid=(b, nkb),
        in_specs=[per_q(KV_LATENT, ATT_HEADS * QB), per_q(IDX_DIM, IDX_HEADS * QB),
                  per_q(IDX_HEADS, QB),
                  per_b(QB, IDX_DIM), per_b(QB, KV_LATENT), per_b(KV_LATENT, QB)] +
                 [_full(a.shape) for a in (km, cm, ctm, bnear, bmeta, bfar, wuvt)],
        out_specs=row,
        out_shape=jax.ShapeDtypeStruct((b, nkb * QB, d), BF16),
        scratch_shapes=[pltpu.VMEM((nkb, QB, QB), I32),
                        pltpu.VMEM((META_ROWS, QB), I32),
                        pltpu.VMEM((nkb, QB, QB), I16),
                        pltpu.VMEM((META_ROWS, QB), I16),
                        pltpu.VMEM((ATT_HEADS, QB), F32),
                        pltpu.VMEM((ATT_HEADS, QB), F32),
                        pltpu.VMEM((ATT_HEADS, KV_LATENT, QB), F32)],
        compiler_params=_params(2),
        name="dsa_core",
    )(qlt, qit, wt, kn, c, ct, km, cm, ctm, bnear, bmeta, bfar, wuvt)


def _rel_bucket(rel):
    nb = REL_BUCKETS // 2
    max_exact = nb // 2
    ret = jnp.where(rel > 0, nb, 0)
    n = jnp.abs(rel)
    nf = jnp.maximum(n, 1).astype(F32)
    large = max_exact + (jnp.log(nf / max_exact) / math.log(REL_MAX_DIST / max_exact)
                         * (nb - max_exact)).astype(I32)
    large = jnp.minimum(large, nb - 1)
    return ret + jnp.where(n < max_exact, n, large)


def _bias_table(rel_bias, rel):
    onehot = (_rel_bucket(rel)[:, :, None] == jnp.arange(REL_BUCKETS, dtype=I32)).astype(F32)
    t = jnp.einsum("kqb,bh->khq", onehot, rel_bias.astype(F32) * LOG2_E,
                   precision=lax.Precision.HIGHEST)
    return t.reshape(rel.shape[0], -1)


def kernel(x, meta, norm_mix, norm_ffn, norm_final, gla_w_in, gla_w_a1, gla_w_a2, gla_b_a,
           gla_g_out, gla_w_out, dsa_w_in, dsa_g_kv, dsa_w_uk, dsa_w_uv, dsa_w_out, rel_bias,
           ffn_w_in, ffn_w_out):
    b, s, d = x.shape
    assert s % QB == 0 and d % LANES == 0
    key_dim = d // 2
    head_dim = d // ATT_HEADS
    d_ff = ffn_w_out.shape[1]
    ff_tile = d_ff // 2
    n_sel = min(TOPK_MAX, s // 4)
    row = lambda a: a.reshape(1, -1).astype(F32)
    bf = lambda a: a.astype(BF16)

    meta_x = jnp.zeros((1, META_ROWS, d), F32).at[0, :N_META].set(meta.astype(F32))
    rows = 512 if s % 512 == 0 else QB

    wa1 = jnp.zeros((d, LANES), F32).at[:, :GLA_GATE_RANK].set(gla_w_a1[0])
    wa2 = jnp.zeros((LANES, key_dim), F32).at[:GLA_GATE_RANK].set(gla_w_a2[0])
    gla_args = (row(norm_mix[0]), bf(gla_w_in[0]), bf(wa1), bf(wa2), row(gla_b_a[0]))
    gout = row(gla_g_out[0])
    s0 = jnp.zeros((GLA_HEADS, d // GLA_HEADS, key_dim // GLA_HEADS), F32)
    ffn0 = (bf(gla_w_out[0]), row(norm_ffn[0]), bf(ffn_w_in[0]), bf(ffn_w_out[0]),
            row(norm_final))

    mq, mk, mv, mr, md = _gla_in(meta_x, *gla_args, rows=META_ROWS, n_valid=N_META)
    mo, s_meta = _gla_scan(mq, mk, mv, mr, md, gout, s0, rows=META_ROWS)
    h_m = _ffn(meta_x, mo, *ffn0, rows=META_ROWS, ff_tile=ff_tile, final_norm=False)

    fq, fk, fv, fr, fd = _gla_in(x, *gla_args, rows=rows, n_valid=None)
    fo, _ = _gla_scan(fq, fk, fv, fr, fd, gout, s_meta[0], rows=rows)
    h_f = _ffn(x, fo, *ffn0, rows=rows, ff_tile=ff_tile, final_norm=False)

    w = dsa_w_in[0]
    s1 = ATT_HEADS * head_dim
    s2 = s1 + KV_LATENT
    s3 = s2 + IDX_HEADS * IDX_DIM
    s4 = s3 + IDX_DIM
    dsa_args = (row(norm_mix[1]), bf(w[:, :s1].T), bf(dsa_w_uk[0]), bf(w[:, s1:s2]),
                bf(w[:, s1:s2].T), row(dsa_g_kv[0]), dsa_g_kv[0].reshape(-1, 1).astype(F32),
                bf(w[:, s2:s3].T), bf(w[:, s3:s4]), bf(w[:, s4:].T))
    _, _, cm, ctm, km, _ = _dsa_in(h_m, *dsa_args, rb=META_ROWS)
    qlt, qit, c, ct, kn, wt = _dsa_in(h_f, *dsa_args, rb=QB)

    ks = jnp.arange(2 * QB, dtype=I32)[:, None] - QB
    qs = jnp.arange(QB, dtype=I32)[None, :]
    bnear = _bias_table(rel_bias, ks - qs)
    bmeta = _bias_table(rel_bias, jnp.arange(META_ROWS, dtype=I32)[:, None] - N_META - qs)
    bfar = _bias_table(rel_bias, jnp.full((1, QB), -(QB + 1), I32))
    wuvt = bf(jnp.transpose(dsa_w_uv[0], (0, 2, 1)))
    att = _dsa_core(qlt, qit, wt, kn, c, ct, km, cm, ctm, bnear, bmeta, bfar, wuvt, n_sel=n_sel)

    ffn1 = (bf(dsa_w_out[0]), row(norm_ffn[1]), bf(ffn_w_in[1]), bf(ffn_w_out[1]),
            row(norm_final))
    return _ffn(h_f, att, *ffn1, rows=rows, ff_tile=ff_tile, final_norm=True).astype(x.dtype)
```

```python
import functools
import math

import jax
import jax.numpy as jnp
from jax import lax
from jax.experimental import pallas as pl
from jax.experimental.pallas import tpu as pltpu

F32 = jnp.float32
BF16 = jnp.bfloat16
I32 = jnp.int32
I16 = jnp.int16

EPS = 1e-6
CHUNK = 64
N_META = 16
GLA_HEADS = 4
GLA_TAU = 16.0
GLA_GATE_RANK = 16
ATT_HEADS = 8
IDX_HEADS = 8
IDX_DIM = 64
KV_LATENT = 256
TOPK_MAX = 256
REL_BUCKETS = 32
REL_MAX_DIST = 128

LANES = 128
SUBLANES = 8
PACKED_ROWS = 16
META_ROWS = 128
QB = 256
FAR_PAIRS = 4
FF_CHUNK = 256
FFN_ROWS = 1024
VMEM_LIMIT_BYTES = 56 * 1024 * 1024

INT_MIN = -(2 ** 31)
HALF = 2 ** 15
NEG_BIG = -1e30
LOG2_E = math.log2(math.e)

_NT = (((1,), (1,)), ((), ()))
_TN = (((0,), (0,)), ((), ()))


def _params(n_grid):
    return pltpu.CompilerParams(
        dimension_semantics=("arbitrary",) * n_grid,
        vmem_limit_bytes=VMEM_LIMIT_BYTES)


def _full(shape):
    nd = len(shape)
    return pl.BlockSpec(shape, lambda *_: (0,) * nd, pipeline_mode=pl.Buffered(1))


def _rms(x, gain):
    return x * lax.rsqrt(jnp.mean(x * x, axis=-1, keepdims=True) + EPS) * gain


def _silu(x):
    return x / (1.0 + jnp.exp(-x))


def _gla_in_kernel(x_ref, gn_ref, w_ref, wa1_ref, wa2_ref, ba_ref,
                   q_ref, kd_ref, v_ref, r_ref, dtot_ref,
                   *, rows, key_dim, d_model, q_scale, n_valid):
    hb = _rms(x_ref[0], gn_ref[...]).astype(BF16)
    o = 0
    q = jnp.dot(hb, w_ref[:, o:o + key_dim], preferred_element_type=F32)
    q_ref[0] = (q * q_scale).astype(BF16)
    o += key_dim
    k = jnp.dot(hb, w_ref[:, o:o + key_dim], preferred_element_type=F32)
    o += key_dim
    v_ref[0] = jnp.dot(hb, w_ref[:, o:o + d_model], preferred_element_type=F32).astype(BF16)
    o += d_model
    r_ref[0] = jnp.dot(hb, w_ref[:, o:o + d_model], preferred_element_type=F32).astype(BF16)
    a = jnp.dot(hb, wa1_ref[...], preferred_element_type=F32).astype(BF16)
    z = jnp.dot(a, wa2_ref[...], preferred_element_type=F32) + ba_ref[...]
    log_sig = jnp.minimum(z, 0.0) - jnp.log(1.0 + jnp.exp(-jnp.abs(z)))
    cum = log_sig * (1.0 / GLA_TAU)
    row = lax.broadcasted_iota(I32, (rows, key_dim), 0)
    if n_valid is not None:
        cum = jnp.where(pl.program_id(1) * rows + row < n_valid, cum, 0.0)
    pos = row % CHUNK
    step = 1
    while step < CHUNK:
        cum = cum + jnp.where(pos >= step, pltpu.roll(cum, step, axis=0), 0.0)
        step *= 2
    n_chunks = rows // CHUNK
    cum = cum.reshape(n_chunks, CHUNK, key_dim)
    tot = cum[:, CHUNK - 1:CHUNK, :]
    kd = k.reshape(n_chunks, CHUNK, key_dim) * jnp.exp(tot - cum)
    kd_ref[0] = kd.reshape(rows, key_dim).astype(BF16)
    dtot_ref[0] = jnp.exp(tot).reshape(n_chunks, key_dim)


def _gla_in(x, gn, w, wa1, wa2, ba, *, rows, n_valid):
    b, s, d = x.shape
    key_dim = d // 2
    dk = key_dim // GLA_HEADS
    kern = functools.partial(_gla_in_kernel, rows=rows, key_dim=key_dim, d_model=d,
                             q_scale=dk ** -0.5, n_valid=n_valid)
    row = lambda width: pl.BlockSpec((1, rows, width), lambda i, j: (i, j, 0))
    return pl.pallas_call(
        kern,
        grid=(b, s // rows),
        in_specs=[row(d), _full(gn.shape), _full(w.shape), _full(wa1.shape),
                  _full(wa2.shape), _full(ba.shape)],
        out_specs=[row(key_dim), row(key_dim), row(d), row(d),
                   pl.BlockSpec((1, rows // CHUNK, key_dim), lambda i, j: (i, j, 0))],
        out_shape=[jax.ShapeDtypeStruct((b, s, key_dim), BF16),
                   jax.ShapeDtypeStruct((b, s, key_dim), BF16),
                   jax.ShapeDtypeStruct((b, s, d), BF16),
                   jax.ShapeDtypeStruct((b, s, d), BF16),
                   jax.ShapeDtypeStruct((b, s // CHUNK, key_dim), F32)],
        compiler_params=_params(2),
        name="gla_in",
    )(x, gn, w, wa1, wa2, ba)


def _gla_scan_kernel(q_ref, kd_ref, v_ref, r_ref, dtot_ref, gout_ref, s0_ref,
                     o_ref, sfin_ref, s_scr, *, rows, dk, dv):
    j = pl.program_id(1)

    @pl.when(j == 0)
    def _():
        s_scr[...] = s0_ref[...]

    for c in range(rows // CHUNK):
        rs = slice(c * CHUNK, (c + 1) * CHUNK)
        for h in range(GLA_HEADS):
            ks = slice(h * dk, (h + 1) * dk)
            vs = slice(h * dv, (h + 1) * dv)
            upd = lax.dot_general(v_ref[0, rs, vs], kd_ref[0, rs, ks], _TN,
                                  preferred_element_type=F32)
            s_new = dtot_ref[0, c:c + 1, ks] * s_scr[h] + upd
            s_scr[h] = s_new
            o = lax.dot_general(q_ref[0, rs, ks], s_new.astype(BF16), _NT,
                                preferred_element_type=F32)
            on = _rms(o, gout_ref[:, vs])
            rr = r_ref[0, rs, vs].astype(F32)
            o_ref[0, rs, vs] = (on * _silu(rr)).astype(BF16)

    @pl.when(j == pl.num_programs(1) - 1)
    def _():
        sfin_ref[0] = s_scr[...]


def _gla_scan(q, kd, v, r, dtot, gout, s0, *, rows):
    b, s, d = v.shape
    key_dim = q.shape[-1]
    dk, dv = key_dim // GLA_HEADS, d // GLA_HEADS
    kern = functools.partial(_gla_scan_kernel, rows=rows, dk=dk, dv=dv)
    row = lambda width: pl.BlockSpec((1, rows, width), lambda i, j: (i, j, 0))
    return pl.pallas_call(
        kern,
        grid=(b, s // rows),
        in_specs=[row(key_dim), row(key_dim), row(d), row(d),
                  pl.BlockSpec((1, rows // CHUNK, key_dim), lambda i, j: (i, j, 0)),
                  _full(gout.shape), _full(s0.shape)],
        out_specs=[row(d), pl.BlockSpec((1, GLA_HEADS, dv, dk), lambda i, j: (i, 0, 0, 0))],
        out_shape=[jax.ShapeDtypeStruct((b, s, d), BF16),
                   jax.ShapeDtypeStruct((b, GLA_HEADS, dv, dk), F32)],
        scratch_shapes=[pltpu.VMEM((GLA_HEADS, dv, dk), F32)],
        compiler_params=_params(2),
        name="gla_scan",
    )(q, kd, v, r, dtot, gout, s0)


def _ffn_kernel(h_ref, a_ref, wp_ref, gn_ref, wi_ref, wo_ref, gf_ref, o_ref, acc_scr,
                *, final_norm, d_ff):
    res = h_ref[0] + jnp.dot(a_ref[0], wp_ref[...], preferred_element_type=F32)
    hb = _rms(res, gn_ref[...]).astype(BF16)
    for t in range(d_ff // FF_CHUNK):
        lo = t * FF_CHUNK
        gate = jnp.dot(hb, wi_ref[:, lo:lo + FF_CHUNK], preferred_element_type=F32)
        up = jnp.dot(hb, wi_ref[:, d_ff + lo:d_ff + lo + FF_CHUNK], preferred_element_type=F32)
        act = (_silu(gate) * up).astype(BF16)
        part = jnp.dot(act, wo_ref[lo:lo + FF_CHUNK, :], preferred_element_type=F32)
        if t == 0:
            acc_scr[...] = res + part
        else:
            acc_scr[...] += part
    y = acc_scr[...]
    if final_norm:
        y = _rms(y, gf_ref[...])
    o_ref[0] = y


def _ffn(h, a, w_proj, gn, w_in, w_out, gf, *, rows, final_norm):
    b, s, d = h.shape
    d_ff = w_out.shape[0]
    assert d_ff % FF_CHUNK == 0
    kern = functools.partial(_ffn_kernel, final_norm=final_norm, d_ff=d_ff)
    row = pl.BlockSpec((1, rows, d), lambda i, j: (i, j, 0))
    return pl.pallas_call(
        kern,
        grid=(b, s // rows),
        in_specs=[row, row, _full(w_proj.shape), _full(gn.shape), _full(w_in.shape),
                  _full(w_out.shape), _full(gf.shape)],
        out_specs=row,
        out_shape=jax.ShapeDtypeStruct((b, s, d), F32),
        scratch_shapes=[pltpu.VMEM((rows, d), F32)],
        compiler_params=_params(2),
        name="ffn",
    )(h, a, w_proj, gn, w_in, w_out, gf)


def _dsa_in_kernel(h_ref, gn_ref, wqt_ref, wuk_ref, wc_ref, wct_ref, gkv_ref, gkvc_ref,
                   wqit_ref, wk_ref, wwt_ref,
                   qlt_ref, qit_ref, c_ref, ct_ref, k_ref, wt_ref, *, rb, head_dim):
    hb = _rms(h_ref[0], gn_ref[...]).astype(BF16)
    qt = lax.dot_general(wqt_ref[...], hb, _NT, preferred_element_type=F32)
    qit = lax.dot_general(wqit_ref[...], hb, _NT, preferred_element_type=F32)
    qit = (qit * IDX_DIM ** -0.5).astype(BF16)
    for h in range(IDX_HEADS):
        qit_ref[0, 0, :, h * rb:(h + 1) * rb] = qit[h * IDX_DIM:(h + 1) * IDX_DIM, :]
    c = jnp.dot(hb, wc_ref[...], preferred_element_type=F32)
    c_ref[0, 0] = _rms(c, gkv_ref[...]).astype(BF16)
    ct = lax.dot_general(wct_ref[...], hb, _NT, preferred_element_type=F32)
    ct = ct * lax.rsqrt(jnp.mean(ct * ct, axis=0, keepdims=True) + EPS) * gkvc_ref[...]
    ct_ref[0, 0] = ct.astype(BF16)
    k_ref[0, 0] = jnp.dot(hb, wk_ref[...], preferred_element_type=F32).astype(BF16)
    wt = lax.dot_general(wwt_ref[...], hb, _NT, preferred_element_type=F32)
    wt_ref[0, 0] = wt * IDX_HEADS ** -0.5
    for h in range(ATT_HEADS):
        qh = qt[h * head_dim:(h + 1) * head_dim, :].astype(BF16)
        ql = jnp.dot(wuk_ref[h], qh, preferred_element_type=F32)
        qlt_ref[0, 0, :, h * rb:(h + 1) * rb] = (ql * (LOG2_E * head_dim ** -0.5)).astype(BF16)


def _dsa_in(h, gn, wqt, wuk, wc, wct, gkv, gkvc, wqit, wk, wwt, *, rb):
    b, s, d = h.shape
    nb = s // rb
    head_dim = wuk.shape[-1]
    kern = functools.partial(_dsa_in_kernel, rb=rb, head_dim=head_dim)
    blk = lambda r, c: pl.BlockSpec((1, 1, r, c), lambda i, j: (i, j, 0, 0))
    shp = lambda r, c, dt: jax.ShapeDtypeStruct((b, nb, r, c), dt)
    return pl.pallas_call(
        kern,
        grid=(b, nb),
        in_specs=[pl.BlockSpec((1, rb, d), lambda i, j: (i, j, 0))] +
                 [_full(a.shape) for a in (gn, wqt, wuk, wc, wct, gkv, gkvc, wqit, wk, wwt)],
        out_specs=[blk(KV_LATENT, ATT_HEADS * rb), blk(IDX_DIM, IDX_HEADS * rb),
                   blk(rb, KV_LATENT), blk(KV_LATENT, rb), blk(rb, IDX_DIM),
                   blk(IDX_HEADS, rb)],
        out_shape=[shp(KV_LATENT, ATT_HEADS * rb, BF16), shp(IDX_DIM, IDX_HEADS * rb, BF16),
                   shp(rb, KV_LATENT, BF16), shp(KV_LATENT, rb, BF16), shp(rb, IDX_DIM, BF16),
                   shp(IDX_HEADS, rb, F32)],
        compiler_params=_params(2),
        name="dsa_in",
    )(h, gn, wqt, wuk, wc, wct, gkv, gkvc, wqit, wk, wwt)


def _sort_key(score):
    score = jnp.where(score == 0.0, 0.0, score)
    bits = pltpu.bitcast(score, I32)
    return bits ^ ((bits >> 31) & 0x7FFFFFFF)


def _dsa_core_kernel(qlt_ref, qit_ref, wt_ref, k_ref, c_ref, ct_ref,
                     km_ref, cm_ref, ctm_ref, bnear_ref, bmeta_ref, bfar_ref,
                     wuvt_ref, o_ref,
                     keys_scr, keysm_scr, k16_scr, k16m_scr, m_scr, l_scr, acc_scr,
                     *, n_sel, head_dim):
    j = pl.program_id(1)
    nh = ATT_HEADS

    def scores(k_blk):
        s8 = jnp.dot(k_blk, qit_ref[0, 0], preferred_element_type=F32)
        acc = None
        for h in range(IDX_HEADS):
            term = jnp.maximum(s8[:, h * QB:(h + 1) * QB], 0.0) * wt_ref[0, 0, h:h + 1, :]
            acc = term if acc is None else acc + term
        return _sort_key(acc)

    def hi16(keys):
        return (keys >> 16).astype(I16)

    def score_pair(kb):
        keys = scores(k_ref[0, pl.ds(kb, 2)].reshape(2 * QB, IDX_DIM))
        keys_scr[pl.ds(kb, 2)] = keys.reshape(2, QB, QB)
        k16_scr[pl.ds(kb, 2)] = hi16(keys).reshape(2, QB, QB)

    def score_body(i, carry):
        score_pair(4 * i)
        score_pair(4 * i + 2)
        return carry

    lax.fori_loop(0, j // 4, score_body, 0)

    @pl.when((j // 2) % 2 == 1)
    def _():
        score_pair(2 * (j // 2 - 1))

    @pl.when(j % 2 == 1)
    def _():
        keys = scores(k_ref[0, j - 1])
        keys_scr[j - 1] = keys
        k16_scr[j - 1] = hi16(keys)

    s_chunk = lax.broadcasted_iota(I32, (QB, QB), 0) // CHUNK
    q_chunk = lax.broadcasted_iota(I32, (QB, QB), 1) // CHUNK
    keys = jnp.where(s_chunk <= q_chunk, scores(k_ref[0, j]), INT_MIN)
    keys_scr[j] = keys
    k16_scr[j] = hi16(keys)
    m_row = lax.broadcasted_iota(I32, (META_ROWS, QB), 0)
    keys = jnp.where(m_row < N_META, scores(km_ref[0, 0]), INT_MIN)
    keysm_scr[...] = keys
    k16m_scr[...] = hi16(keys)

    def count(blocks_ref, meta_ref, rows, dtype, hit):
        zero = jnp.zeros((rows, QB), dtype)
        one = jnp.ones((), dtype)

        def add_rows(load, n_rows, accs):
            accs = list(accs)
            for r in range(n_rows // rows):
                a = accs[r % len(accs)]
                accs[r % len(accs)] = jnp.where(hit(load(r)), a + one, a)
            return tuple(accs)

        def kb_body(kb, accs):
            return add_rows(lambda r: blocks_ref[kb, r * rows:(r + 1) * rows, :], QB, accs)

        accs = lax.fori_loop(0, j + 1, kb_body, (zero,) * 4)
        accs = add_rows(lambda r: meta_ref[r * rows:(r + 1) * rows, :], META_ROWS, accs)
        tot = (accs[0] + accs[1]) + (accs[2] + accs[3])
        return jnp.sum(tot.astype(I32), axis=0, keepdims=True)

    def to16(v):
        return jnp.broadcast_to(v, (PACKED_ROWS, QB)).astype(I16)

    def radix16(need, n_min):
        def bit_body(i, carry):
            res_u, n_ge = carry
            cand_u = res_u | jnp.left_shift(jnp.int32(1), 15 - i)
            cand = to16(cand_u - HALF)
            cnt = count(k16_scr, k16m_scr, PACKED_ROWS, I16, lambda e: e >= cand)
            keep = cnt >= need
            return jnp.where(keep, cand_u, res_u), jnp.where(keep, cnt, n_ge)

        return lax.fori_loop(0, 16, bit_body, (jnp.zeros((1, QB), I32), n_min))

    n_entries = jnp.full((1, QB), META_ROWS, I32) + (j + 1) * QB
    hi_u, n_ge_hi = radix16(n_sel, n_entries)
    hi_sel = hi_u - HALF
    hi_sel16 = to16(hi_sel)
    above = count(k16_scr, k16m_scr, PACKED_ROWS, I16, lambda e: e > hi_sel16)

    def low_half(keys):
        lo = jnp.where((keys >> 16) == hi_sel, (keys & (2 * HALF - 1)) - HALF, -HALF)
        return lo.astype(I16)

    def low_body(kb, carry):
        k16_scr[kb] = low_half(keys_scr[kb])
        return carry

    lax.fori_loop(0, j + 1, low_body, 0)
    k16m_scr[...] = low_half(keysm_scr[...])
    lo_sel, n_ge_lo = radix16(n_sel - above, n_ge_hi - above)
    tau = jnp.maximum(jnp.left_shift(hi_sel, 16) | lo_sel, INT_MIN + 1)

    at_least = jnp.where(hi_u > 0, above + n_ge_lo, 0)

    @pl.when(jnp.max(at_least) > n_sel)
    def _():
        tau8 = jnp.broadcast_to(tau, (SUBLANES, QB))
        quota = n_sel - count(keys_scr, keysm_scr, SUBLANES, I32, lambda e: e > tau8)

        def demote(ref, n, seen):
            keys = ref[...]
            tie = keys == tau
            tri = (lax.broadcasted_iota(I32, (n, n), 0) >=
                   lax.broadcasted_iota(I32, (n, n), 1)).astype(BF16)
            rank = jnp.dot(tri, jnp.where(tie, 1.0, 0.0).astype(BF16),
                           preferred_element_type=F32).astype(I32)
            ref[...] = jnp.where(tie & (seen + rank > quota), tau - 1, keys)
            return seen + rank[n - 1:n, :]

        seen = demote(keysm_scr, META_ROWS, jnp.zeros((1, QB), I32))
        lax.fori_loop(0, j + 1, lambda kb, seen: demote(keys_scr.at[kb], QB, seen), seen)

    m_scr[...] = jnp.full(m_scr.shape, NEG_BIG, F32)
    l_scr[...] = jnp.zeros_like(l_scr)
    acc_scr[...] = jnp.zeros_like(acc_scr)

    def cap_of(keys):
        return jnp.where(keys >= tau, -NEG_BIG, NEG_BIG)

    def attend(c_blk, ct_blk, cap, bias_tab=None, bias_row=None):
        for h in range(nh):
            hs = slice(h * QB, (h + 1) * QB)
            s = jnp.dot(c_blk, qlt_ref[0, 0, :, hs], preferred_element_type=F32)
            if bias_tab is not None:
                s = s + bias_tab[:, hs]
            s = jnp.minimum(s, cap)
            blk_max = jnp.max(s, axis=0, keepdims=True)
            m_prev = m_scr[h:h + 1, :]
            if bias_row is not None:
                b = bias_row[:, hs]
                m_new = jnp.maximum(m_prev, blk_max + b)
                shift = m_new - b
            else:
                m_new = jnp.maximum(m_prev, blk_max)
                shift = m_new
            alpha = jnp.exp2(m_prev - m_new)
            p = jnp.exp2(s - shift)
            l_scr[h:h + 1, :] = alpha * l_scr[h:h + 1, :] + jnp.sum(p, axis=0, keepdims=True)
            acc_scr[h] = alpha * acc_scr[h] + jnp.dot(ct_blk, p.astype(BF16),
                                                      preferred_element_type=F32)
            m_scr[h:h + 1, :] = m_new

    def attend_pair(kb, **bias):
        attend(c_ref[0, pl.ds(kb, 2)].reshape(2 * QB, KV_LATENT),
               jnp.concatenate([ct_ref[0, kb], ct_ref[0, kb + 1]], axis=1),
               cap_of(keys_scr[pl.ds(kb, 2)].reshape(2 * QB, QB)), **bias)

    n_far = jnp.maximum(j - 1, 0)
    n_pairs = n_far // 2

    def far_pairs(first, count):
        for p in range(count):
            attend_pair(first + 2 * p, bias_row=bfar_ref)

    def far_body(i, carry):
        far_pairs(2 * FAR_PAIRS * i, FAR_PAIRS)
        return carry

    lax.fori_loop(0, n_pairs // FAR_PAIRS, far_body, 0)
    left = n_pairs % FAR_PAIRS
    done = 2 * (n_pairs - left)
    width = FAR_PAIRS // 2
    while width >= 1:
        @pl.when((left // width) % 2 == 1)
        def _(width=width, at=done + 2 * width * (left // (2 * width)) * 2):
            far_pairs(at, width)

        width //= 2

    @pl.when(n_far % 2 == 1)
    def _():
        attend(c_ref[0, n_far - 1], ct_ref[0, n_far - 1], cap_of(keys_scr[n_far - 1]),
               bias_row=bfar_ref)

    @pl.when(j >= 1)
    def _():
        attend_pair(j - 1, bias_tab=bnear_ref)
        attend(cm_ref[0, 0], ctm_ref[0, 0], cap_of(keysm_scr[...]), bias_row=bfar_ref)

    @pl.when(j == 0)
    def _():
        attend(c_ref[0, 0], ct_ref[0, 0], cap_of(keys_scr[0]), bias_tab=bnear_ref.at[QB:, :])
        attend(cm_ref[0, 0], ctm_ref[0, 0], cap_of(keysm_scr[...]), bias_tab=bmeta_ref)

    outs = []
    for h in range(nh):
        u = (acc_scr[h] * (1.0 / l_scr[h:h + 1, :])).astype(BF16)
        outs.append(jnp.dot(wuvt_ref[h], u, preferred_element_type=F32))
    o_t = jnp.concatenate(outs, axis=0)
    o_ref[0] = o_t.T.astype(BF16)


def _dsa_core(qlt, qit, wt, kn, c, ct, km, cm, ctm, bnear, bmeta, bfar, wuvt, *, n_sel):
    b, nkb = c.shape[:2]
    d = wuvt.shape[0] * wuvt.shape[1]
    head_dim = wuvt.shape[1]
    kern = functools.partial(_dsa_core_kernel, n_sel=n_sel, head_dim=head_dim)
    per_q = lambda r, cdim: pl.BlockSpec((1, 1, r, cdim), lambda i, j: (i, j, 0, 0))
    per_b = lambda r, cdim: pl.BlockSpec((1, nkb, r, cdim), lambda i, j: (i, 0, 0, 0),
                                         pipeline_mode=pl.Buffered(1))
    row = pl.BlockSpec((1, QB, d), lambda i, j: (i, j, 0))
    return pl.pallas_call(
        kern,
        grid=(b, nkb),
        in_specs=[per_q(KV_LATENT, ATT_HEADS * QB), per_q(IDX_DIM, IDX_HEADS * QB),
                  per_q(IDX_HEADS, QB),
                  per_b(QB, IDX_DIM), per_b(QB, KV_LATENT), per_b(KV_LATENT, QB)] +
                 [_full(a.shape) for a in (km, cm, ctm, bnear, bmeta, bfar, wuvt)],
        out_specs=row,
        out_shape=jax.ShapeDtypeStruct((b, nkb * QB, d), BF16),
        scratch_shapes=[pltpu.VMEM((nkb, QB, QB), I32),
                        pltpu.VMEM((META_ROWS, QB), I32),
                        pltpu.VMEM((nkb, QB, QB), I16),
                        pltpu.VMEM((META_ROWS, QB), I16),
                        pltpu.VMEM((ATT_HEADS, QB), F32),
                        pltpu.VMEM((ATT_HEADS, QB), F32),
                        pltpu.VMEM((ATT_HEADS, KV_LATENT, QB), F32)],
        compiler_params=_params(2),
        name="dsa_core",
    )(qlt, qit, wt, kn, c, ct, km, cm, ctm, bnear, bmeta, bfar, wuvt)


def _rel_bucket(rel):
    nb = REL_BUCKETS // 2
    max_exact = nb // 2
    ret = jnp.where(rel > 0, nb, 0)
    n = jnp.abs(rel)
    nf = jnp.maximum(n, 1).astype(F32)
    large = max_exact + (jnp.log(nf / max_exact) / math.log(REL_MAX_DIST / max_exact)
                         * (nb - max_exact)).astype(I32)
    large = jnp.minimum(large, nb - 1)
    return ret + jnp.where(n < max_exact, n, large)


def _bias_table(rel_bias, rel):
    onehot = (_rel_bucket(rel)[:, :, None] == jnp.arange(REL_BUCKETS, dtype=I32)).astype(F32)
    t = jnp.einsum("kqb,bh->khq", onehot, rel_bias.astype(F32) * LOG2_E,
                   precision=lax.Precision.HIGHEST)
    return t.reshape(rel.shape[0], -1)


def kernel(x, meta, norm_mix, norm_ffn, norm_final, gla_w_in, gla_w_a1, gla_w_a2, gla_b_a,
           gla_g_out, gla_w_out, dsa_w_in, dsa_g_kv, dsa_w_uk, dsa_w_uv, dsa_w_out, rel_bias,
           ffn_w_in, ffn_w_out):
    b, s, d = x.shape
    assert s % QB == 0 and d % LANES == 0
    key_dim = d // 2
    head_dim = d // ATT_HEADS
    d_ff = ffn_w_out.shape[1]
    n_sel = min(TOPK_MAX, s // 4)
    row = lambda a: a.reshape(1, -1).astype(F32)
    bf = lambda a: a.astype(BF16)

    meta_x = jnp.zeros((1, META_ROWS, d), F32).at[0, :N_META].set(meta.astype(F32))
    rows = 512 if s % 512 == 0 else QB
    ffn_rows = FFN_ROWS if s % FFN_ROWS == 0 else rows

    wa1 = jnp.zeros((d, LANES), F32).at[:, :GLA_GATE_RANK].set(gla_w_a1[0])
    wa2 = jnp.zeros((LANES, key_dim), F32).at[:GLA_GATE_RANK].set(gla_w_a2[0])
    gla_args = (row(norm_mix[0]), bf(gla_w_in[0]), bf(wa1), bf(wa2), row(gla_b_a[0]))
    gout = row(gla_g_out[0])
    s0 = jnp.zeros((GLA_HEADS, d // GLA_HEADS, key_dim // GLA_HEADS), F32)
    ffn0 = (bf(gla_w_out[0]), row(norm_ffn[0]), bf(ffn_w_in[0]), bf(ffn_w_out[0]),
            row(norm_final))

    mq, mk, mv, mr, md = _gla_in(meta_x, *gla_args, rows=META_ROWS, n_valid=N_META)
    mo, s_meta = _gla_scan(mq, mk, mv, mr, md, gout, s0, rows=META_ROWS)
    h_m = _ffn(meta_x, mo, *ffn0, rows=META_ROWS, final_norm=False)

    fq, fk, fv, fr, fd = _gla_in(x, *gla_args, rows=rows, n_valid=None)
    fo, _ = _gla_scan(fq, fk, fv, fr, fd, gout, s_meta[0], rows=rows)
    h_f = _ffn(x, fo, *ffn0, rows=ffn_rows, final_norm=False)

    w = dsa_w_in[0]
    s1 = ATT_HEADS * head_dim
    s2 = s1 + KV_LATENT
    s3 = s2 + IDX_HEADS * IDX_DIM
    s4 = s3 + IDX_DIM
    dsa_args = (row(norm_mix[1]), bf(w[:, :s1].T), bf(dsa_w_uk[0]), bf(w[:, s1:s2]),
                bf(w[:, s1:s2].T), row(dsa_g_kv[0]), dsa_g_kv[0].reshape(-1, 1).astype(F32),
                bf(w[:, s2:s3].T), bf(w[:, s3:s4]), bf(w[:, s4:].T))
    _, _, cm, ctm, km, _ = _dsa_in(h_m, *dsa_args, rb=META_ROWS)
    qlt, qit, c, ct, kn, wt = _dsa_in(h_f, *dsa_args, rb=QB)

    ks = jnp.arange(2 * QB, dtype=I32)[:, None] - QB
    qs = jnp.arange(QB, dtype=I32)[None, :]
    bnear = _bias_table(rel_bias, ks - qs)
    bmeta = _bias_table(rel_bias, jnp.arange(META_ROWS, dtype=I32)[:, None] - N_META - qs)
    bfar = _bias_table(rel_bias, jnp.full((1, QB), -(QB + 1), I32))
    wuvt = bf(jnp.transpose(dsa_w_uv[0], (0, 2, 1)))
    att = _dsa_core(qlt, qit, wt, kn, c, ct, km, cm, ctm, bnear, bmeta, bfar, wuvt, n_sel=n_sel)

    ffn1 = (bf(dsa_w_out[0]), row(norm_ffn[1]), bf(ffn_w_in[1]), bf(ffn_w_out[1]),
            row(norm_final))
    return _ffn(h_f, att, *ffn1, rows=ffn_rows, final_norm=True).astype(x.dtype)
```

```python
import functools
import math

import jax
import jax.numpy as jnp
from jax import lax
from jax.experimental import pallas as pl
from jax.experimental.pallas import tpu as pltpu

F32 = jnp.float32
BF16 = jnp.bfloat16
I32 = jnp.int32
I16 = jnp.int16

EPS = 1e-6
CHUNK = 64
N_META = 16
GLA_HEADS = 4
GLA_TAU = 16.0
GLA_GATE_RANK = 16
ATT_HEADS = 8
IDX_HEADS = 8
IDX_DIM = 64
KV_LATENT = 256
TOPK_MAX = 256
REL_BUCKETS = 32
REL_MAX_DIST = 128

LANES = 128
SUBLANES = 8
PACKED_ROWS = 16
META_ROWS = 128
QB = 256
FAR_PAIRS = 4
FF_CHUNK = 256
FFN_ROWS = 1024
VMEM_LIMIT_BYTES = 56 * 1024 * 1024

INT_MIN = -(2 ** 31)
HALF_BITS = 16
HALF = 2 ** (HALF_BITS - 1)
NEG_BIG = -1e30
LOG2_E = math.log2(math.e)

_NT = (((1,), (1,)), ((), ()))
_TN = (((0,), (0,)), ((), ()))


def _params(n_grid):
    return pltpu.CompilerParams(
        dimension_semantics=("arbitrary",) * n_grid,
        vmem_limit_bytes=VMEM_LIMIT_BYTES)


def _full(shape):
    nd = len(shape)
    return pl.BlockSpec(shape, lambda *_: (0,) * nd, pipeline_mode=pl.Buffered(1))


def _rms(x, gain):
    return x * lax.rsqrt(jnp.mean(x * x, axis=-1, keepdims=True) + EPS) * gain


def _silu(x):
    return x / (1.0 + jnp.exp(-x))


def _gla_in_kernel(x_ref, gn_ref, w_ref, wa1_ref, wa2_ref, ba_ref,
                   q_ref, kd_ref, v_ref, r_ref, dtot_ref,
                   *, rows, key_dim, d_model, q_scale, n_valid):
    hb = _rms(x_ref[0], gn_ref[...]).astype(BF16)
    a = jnp.dot(hb, wa1_ref[...], preferred_element_type=F32).astype(BF16)
    z = jnp.dot(a, wa2_ref[...], preferred_element_type=F32) + ba_ref[...]
    k = jnp.dot(hb, w_ref[:, key_dim:2 * key_dim], preferred_element_type=F32)
    log_sig = jnp.minimum(z, 0.0) - jnp.log(1.0 + jnp.exp(-jnp.abs(z)))
    cum = log_sig * (1.0 / GLA_TAU)
    row = lax.broadcasted_iota(I32, (rows, key_dim), 0)
    if n_valid is not None:
        cum = jnp.where(pl.program_id(1) * rows + row < n_valid, cum, 0.0)
    pos = row % CHUNK
    step = 1
    while step < CHUNK:
        cum = cum + jnp.where(pos >= step, pltpu.roll(cum, step, axis=0), 0.0)
        step *= 2
    n_chunks = rows // CHUNK
    cum = cum.reshape(n_chunks, CHUNK, key_dim)
    tot = cum[:, CHUNK - 1:CHUNK, :]
    kd = k.reshape(n_chunks, CHUNK, key_dim) * jnp.exp(tot - cum)
    kd_ref[0] = kd.reshape(rows, key_dim).astype(BF16)
    dtot_ref[0] = jnp.exp(tot).reshape(n_chunks, key_dim)
    q = jnp.dot(hb, w_ref[:, :key_dim], preferred_element_type=F32)
    q_ref[0] = (q * q_scale).astype(BF16)
    o = 2 * key_dim
    v_ref[0] = jnp.dot(hb, w_ref[:, o:o + d_model], preferred_element_type=F32).astype(BF16)
    o += d_model
    r_ref[0] = jnp.dot(hb, w_ref[:, o:o + d_model], preferred_element_type=F32).astype(BF16)


def _gla_in(x, gn, w, wa1, wa2, ba, *, rows, n_valid):
    b, s, d = x.shape
    key_dim = d // 2
    dk = key_dim // GLA_HEADS
    kern = functools.partial(_gla_in_kernel, rows=rows, key_dim=key_dim, d_model=d,
                             q_scale=dk ** -0.5, n_valid=n_valid)
    row = lambda width: pl.BlockSpec((1, rows, width), lambda i, j: (i, j, 0))
    return pl.pallas_call(
        kern,
        grid=(b, s // rows),
        in_specs=[row(d), _full(gn.shape), _full(w.shape), _full(wa1.shape),
                  _full(wa2.shape), _full(ba.shape)],
        out_specs=[row(key_dim), row(key_dim), row(d), row(d),
                   pl.BlockSpec((1, rows // CHUNK, key_dim), lambda i, j: (i, j, 0))],
        out_shape=[jax.ShapeDtypeStruct((b, s, key_dim), BF16),
                   jax.ShapeDtypeStruct((b, s, key_dim), BF16),
                   jax.ShapeDtypeStruct((b, s, d), BF16),
                   jax.ShapeDtypeStruct((b, s, d), BF16),
                   jax.ShapeDtypeStruct((b, s // CHUNK, key_dim), F32)],
        compiler_params=_params(2),
        name="gla_in",
    )(x, gn, w, wa1, wa2, ba)


def _gla_scan_kernel(q_ref, kd_ref, v_ref, r_ref, dtot_ref, gout_ref, s0_ref,
                     o_ref, sfin_ref, s_scr, *, rows, dk, dv):
    j = pl.program_id(1)

    @pl.when(j == 0)
    def _():
        s_scr[...] = s0_ref[...]

    for c in range(rows // CHUNK):
        rs = slice(c * CHUNK, (c + 1) * CHUNK)
        for h in range(GLA_HEADS):
            ks = slice(h * dk, (h + 1) * dk)
            vs = slice(h * dv, (h + 1) * dv)
            upd = lax.dot_general(v_ref[0, rs, vs], kd_ref[0, rs, ks], _TN,
                                  preferred_element_type=F32)
            s_new = dtot_ref[0, c:c + 1, ks] * s_scr[h] + upd
            s_scr[h] = s_new
            o = lax.dot_general(q_ref[0, rs, ks], s_new.astype(BF16), _NT,
                                preferred_element_type=F32)
            on = _rms(o, gout_ref[:, vs])
            rr = r_ref[0, rs, vs].astype(F32)
            o_ref[0, rs, vs] = (on * _silu(rr)).astype(BF16)

    @pl.when(j == pl.num_programs(1) - 1)
    def _():
        sfin_ref[0] = s_scr[...]


def _gla_scan(q, kd, v, r, dtot, gout, s0, *, rows):
    b, s, d = v.shape
    key_dim = q.shape[-1]
    dk, dv = key_dim // GLA_HEADS, d // GLA_HEADS
    kern = functools.partial(_gla_scan_kernel, rows=rows, dk=dk, dv=dv)
    row = lambda width: pl.BlockSpec((1, rows, width), lambda i, j: (i, j, 0))
    return pl.pallas_call(
        kern,
        grid=(b, s // rows),
        in_specs=[row(key_dim), row(key_dim), row(d), row(d),
                  pl.BlockSpec((1, rows // CHUNK, key_dim), lambda i, j: (i, j, 0)),
                  _full(gout.shape), _full(s0.shape)],
        out_specs=[row(d), pl.BlockSpec((1, GLA_HEADS, dv, dk), lambda i, j: (i, 0, 0, 0))],
        out_shape=[jax.ShapeDtypeStruct((b, s, d), BF16),
                   jax.ShapeDtypeStruct((b, GLA_HEADS, dv, dk), F32)],
        scratch_shapes=[pltpu.VMEM((GLA_HEADS, dv, dk), F32)],
        compiler_params=_params(2),
        name="gla_scan",
    )(q, kd, v, r, dtot, gout, s0)


def _ffn_kernel(h_ref, a_ref, wp_ref, gn_ref, wi_ref, wo_ref, gf_ref, o_ref, acc_scr,
                *, final_norm, d_ff):
    res = h_ref[0] + jnp.dot(a_ref[0], wp_ref[...], preferred_element_type=F32)
    hb = _rms(res, gn_ref[...]).astype(BF16)
    for t in range(d_ff // FF_CHUNK):
        lo = t * FF_CHUNK
        gate = jnp.dot(hb, wi_ref[:, lo:lo + FF_CHUNK], preferred_element_type=F32)
        up = jnp.dot(hb, wi_ref[:, d_ff + lo:d_ff + lo + FF_CHUNK], preferred_element_type=F32)
        act = (_silu(gate) * up).astype(BF16)
        part = jnp.dot(act, wo_ref[lo:lo + FF_CHUNK, :], preferred_element_type=F32)
        if t == 0:
            acc_scr[...] = res + part
        else:
            acc_scr[...] += part
    y = acc_scr[...]
    if final_norm:
        y = _rms(y, gf_ref[...])
    o_ref[0] = y


def _ffn(h, a, w_proj, gn, w_in, w_out, gf, *, rows, final_norm):
    b, s, d = h.shape
    d_ff = w_out.shape[0]
    assert d_ff % FF_CHUNK == 0
    kern = functools.partial(_ffn_kernel, final_norm=final_norm, d_ff=d_ff)
    row = pl.BlockSpec((1, rows, d), lambda i, j: (i, j, 0))
    return pl.pallas_call(
        kern,
        grid=(b, s // rows),
        in_specs=[row, row, _full(w_proj.shape), _full(gn.shape), _full(w_in.shape),
                  _full(w_out.shape), _full(gf.shape)],
        out_specs=row,
        out_shape=jax.ShapeDtypeStruct((b, s, d), F32),
        scratch_shapes=[pltpu.VMEM((rows, d), F32)],
        compiler_params=_params(2),
        name="ffn",
    )(h, a, w_proj, gn, w_in, w_out, gf)


def _dsa_in_kernel(h_ref, gn_ref, wqt_ref, wuk_ref, wc_ref, wct_ref, gkv_ref, gkvc_ref,
                   wqit_ref, wk_ref, wwt_ref,
                   qlt_ref, qit_ref, c_ref, ct_ref, k_ref, wt_ref, *, rb, head_dim):
    hb = _rms(h_ref[0], gn_ref[...]).astype(BF16)
    qt = lax.dot_general(wqt_ref[...], hb, _NT, preferred_element_type=F32)
    qit = lax.dot_general(wqit_ref[...], hb, _NT, preferred_element_type=F32)
    qit = (qit * IDX_DIM ** -0.5).astype(BF16)
    for h in range(IDX_HEADS):
        qit_ref[0, 0, :, h * rb:(h + 1) * rb] = qit[h * IDX_DIM:(h + 1) * IDX_DIM, :]
    c = jnp.dot(hb, wc_ref[...], preferred_element_type=F32)
    c_ref[0, 0] = _rms(c, gkv_ref[...]).astype(BF16)
    ct = lax.dot_general(wct_ref[...], hb, _NT, preferred_element_type=F32)
    ct = ct * lax.rsqrt(jnp.mean(ct * ct, axis=0, keepdims=True) + EPS) * gkvc_ref[...]
    ct_ref[0, 0] = ct.astype(BF16)
    k_ref[0, 0] = jnp.dot(hb, wk_ref[...], preferred_element_type=F32).astype(BF16)
    wt = lax.dot_general(wwt_ref[...], hb, _NT, preferred_element_type=F32)
    wt_ref[0, 0] = wt * IDX_HEADS ** -0.5
    for h in range(ATT_HEADS):
        qh = qt[h * head_dim:(h + 1) * head_dim, :].astype(BF16)
        ql = jnp.dot(wuk_ref[h], qh, preferred_element_type=F32)
        qlt_ref[0, 0, :, h * rb:(h + 1) * rb] = (ql * (LOG2_E * head_dim ** -0.5)).astype(BF16)


def _dsa_in(h, gn, wqt, wuk, wc, wct, gkv, gkvc, wqit, wk, wwt, *, rb):
    b, s, d = h.shape
    nb = s // rb
    head_dim = wuk.shape[-1]
    kern = functools.partial(_dsa_in_kernel, rb=rb, head_dim=head_dim)
    blk = lambda r, c: pl.BlockSpec((1, 1, r, c), lambda i, j: (i, j, 0, 0))
    shp = lambda r, c, dt: jax.ShapeDtypeStruct((b, nb, r, c), dt)
    return pl.pallas_call(
        kern,
        grid=(b, nb),
        in_specs=[pl.BlockSpec((1, rb, d), lambda i, j: (i, j, 0))] +
                 [_full(a.shape) for a in (gn, wqt, wuk, wc, wct, gkv, gkvc, wqit, wk, wwt)],
        out_specs=[blk(KV_LATENT, ATT_HEADS * rb), blk(IDX_DIM, IDX_HEADS * rb),
                   blk(rb, KV_LATENT), blk(KV_LATENT, rb), blk(rb, IDX_DIM),
                   blk(IDX_HEADS, rb)],
        out_shape=[shp(KV_LATENT, ATT_HEADS * rb, BF16), shp(IDX_DIM, IDX_HEADS * rb, BF16),
                   shp(rb, KV_LATENT, BF16), shp(KV_LATENT, rb, BF16), shp(rb, IDX_DIM, BF16),
                   shp(IDX_HEADS, rb, F32)],
        compiler_params=_params(2),
        name="dsa_in",
    )(h, gn, wqt, wuk, wc, wct, gkv, gkvc, wqit, wk, wwt)


def _sort_key(score):
    score = jnp.where(score == 0.0, 0.0, score)
    bits = pltpu.bitcast(score, I32)
    return bits ^ ((bits >> 31) & 0x7FFFFFFF)


def _dsa_core_kernel(qlt_ref, qit_ref, wt_ref, k_ref, c_ref, ct_ref,
                     km_ref, cm_ref, ctm_ref, bnear_ref, bmeta_ref, bfar_ref,
                     wuvt_ref, o_ref,
                     keys_scr, keysm_scr, k16_scr, k16m_scr, m_scr, l_scr, acc_scr,
                     *, n_sel, head_dim):
    j = pl.program_id(1)
    nh = ATT_HEADS

    def scores(k_blk):
        s8 = jnp.dot(k_blk, qit_ref[0, 0], preferred_element_type=F32)
        acc = None
        for h in range(IDX_HEADS):
            term = jnp.maximum(s8[:, h * QB:(h + 1) * QB], 0.0) * wt_ref[0, 0, h:h + 1, :]
            acc = term if acc is None else acc + term
        return _sort_key(acc)

    def hi16(keys):
        return (keys >> HALF_BITS).astype(I16)

    def score_pair(kb):
        keys = scores(k_ref[0, pl.ds(kb, 2)].reshape(2 * QB, IDX_DIM))
        keys_scr[pl.ds(kb, 2)] = keys.reshape(2, QB, QB)
        k16_scr[pl.ds(kb, 2)] = hi16(keys).reshape(2, QB, QB)

    def score_body(i, carry):
        score_pair(4 * i)
        score_pair(4 * i + 2)
        return carry

    lax.fori_loop(0, j // 4, score_body, 0)

    @pl.when((j // 2) % 2 == 1)
    def _():
        score_pair(2 * (j // 2 - 1))

    @pl.when(j % 2 == 1)
    def _():
        keys = scores(k_ref[0, j - 1])
        keys_scr[j - 1] = keys
        k16_scr[j - 1] = hi16(keys)

    s_chunk = lax.broadcasted_iota(I32, (QB, QB), 0) // CHUNK
    q_chunk = lax.broadcasted_iota(I32, (QB, QB), 1) // CHUNK
    keys = jnp.where(s_chunk <= q_chunk, scores(k_ref[0, j]), INT_MIN)
    keys_scr[j] = keys
    k16_scr[j] = hi16(keys)
    m_row = lax.broadcasted_iota(I32, (META_ROWS, QB), 0)
    keys = jnp.where(m_row < N_META, scores(km_ref[0, 0]), INT_MIN)
    keysm_scr[...] = keys
    k16m_scr[...] = hi16(keys)

    @pl.when(j % 2 == 0)
    def _():
        k16_scr[j + 1] = jnp.full((QB, QB), -HALF, I16)

    def count(blocks_ref, meta_ref, rows, dtype, hit, *, paired):
        zero = jnp.zeros((rows, QB), dtype)
        one = jnp.ones((), dtype)

        def add_rows(load, n_rows, accs):
            accs = list(accs)
            for r in range(n_rows // rows):
                a = accs[r % len(accs)]
                accs[r % len(accs)] = jnp.where(hit(load(r)), a + one, a)
            return tuple(accs)

        def block(kb, accs):
            return add_rows(lambda r: blocks_ref[kb, r * rows:(r + 1) * rows, :], QB, accs)

        if paired:
            accs = lax.fori_loop(0, (j + 2) // 2,
                                 lambda i, accs: block(2 * i + 1, block(2 * i, accs)), (zero,) * 4)
        else:
            accs = lax.fori_loop(0, j + 1, block, (zero,) * 4)
        accs = add_rows(lambda r: meta_ref[r * rows:(r + 1) * rows, :], META_ROWS, accs)
        tot = (accs[0] + accs[1]) + (accs[2] + accs[3])
        return jnp.sum(tot.astype(I32), axis=0, keepdims=True)

    def count16(hit):
        return count(k16_scr, k16m_scr, PACKED_ROWS, I16, hit, paired=True)

    def to16(v):
        return jnp.broadcast_to(v, (PACKED_ROWS, QB)).astype(I16)

    def radix16(need, res_u, n_ge, n_bits):
        def bit_body(i, carry):
            res_u, n_ge = carry
            cand_u = res_u | jnp.left_shift(jnp.int32(1), n_bits - 1 - i)
            cand = to16(cand_u - HALF)
            cnt = count16(lambda e: e >= cand)
            keep = cnt >= need
            return jnp.where(keep, cand_u, res_u), jnp.where(keep, cnt, n_ge)

        return lax.fori_loop(0, n_bits, bit_body, (res_u, n_ge))

    max16 = lambda a, b: jnp.where(a > b, a, b)
    gmax = lax.fori_loop(0, j + 1, lambda kb, m: max16(m, k16_scr[kb]),
                         jnp.full((QB, QB), -HALF, I16))
    gmax = jnp.concatenate([max16(gmax[:META_ROWS], k16m_scr[...]), gmax[META_ROWS:]],
                           axis=0).astype(I32)
    top_u = jnp.max(gmax, axis=0, keepdims=True) + HALF
    diff = top_u ^ (jnp.min(gmax, axis=0, keepdims=True) + HALF)
    width = jnp.where(diff > 0, (pltpu.bitcast(diff.astype(F32), I32) >> 23) - 126, 0)
    n_bits = jnp.max(width)
    hi_u = jnp.left_shift(jnp.right_shift(top_u, n_bits), n_bits)
    start = to16(hi_u - HALF)
    hi_u, n_ge_hi = radix16(n_sel, hi_u, count16(lambda e: e >= start), n_bits)
    hi_sel = hi_u - HALF
    hi_sel16 = to16(hi_sel)
    above = count16(lambda e: e > hi_sel16)

    def low_half(keys):
        lo = jnp.where((keys >> HALF_BITS) == hi_sel, (keys & (2 * HALF - 1)) - HALF, -HALF)
        return lo.astype(I16)

    def low_body(kb, carry):
        k16_scr[kb] = low_half(keys_scr[kb])
        return carry

    lax.fori_loop(0, j + 1, low_body, 0)
    k16m_scr[...] = low_half(keysm_scr[...])
    lo_sel, n_ge_lo = radix16(n_sel - above, jnp.zeros((1, QB), I32), n_ge_hi - above,
                              HALF_BITS)
    tau = jnp.maximum(jnp.left_shift(hi_sel, HALF_BITS) | lo_sel, INT_MIN + 1)

    at_least = jnp.where(hi_u > 0, above + n_ge_lo, 0)

    @pl.when(jnp.max(at_least) > n_sel)
    def _():
        tau8 = jnp.broadcast_to(tau, (SUBLANES, QB))
        quota = n_sel - count(keys_scr, keysm_scr, SUBLANES, I32, lambda e: e > tau8,
                              paired=False)

        def demote(ref, n, seen):
            keys = ref[...]
            tie = keys == tau
            tri = (lax.broadcasted_iota(I32, (n, n), 0) >=
                   lax.broadcasted_iota(I32, (n, n), 1)).astype(BF16)
            rank = jnp.dot(tri, jnp.where(tie, 1.0, 0.0).astype(BF16),
                           preferred_element_type=F32).astype(I32)
            ref[...] = jnp.where(tie & (seen + rank > quota), tau - 1, keys)
            return seen + rank[n - 1:n, :]

        seen = demote(keysm_scr, META_ROWS, jnp.zeros((1, QB), I32))
        lax.fori_loop(0, j + 1, lambda kb, seen: demote(keys_scr.at[kb], QB, seen), seen)

    m_scr[...] = jnp.full(m_scr.shape, NEG_BIG, F32)
    l_scr[...] = jnp.zeros_like(l_scr)
    acc_scr[...] = jnp.zeros_like(acc_scr)

    def cap_of(keys):
        return jnp.where(keys >= tau, -NEG_BIG, NEG_BIG)

    def attend(c_blk, ct_blk, cap, bias_tab=None, bias_row=None):
        for h in range(nh):
            hs = slice(h * QB, (h + 1) * QB)
            s = jnp.dot(c_blk, qlt_ref[0, 0, :, hs], preferred_element_type=F32)
            if bias_tab is not None:
                s = s + bias_tab[:, hs]
            s = jnp.minimum(s, cap)
            blk_max = jnp.max(s, axis=0, keepdims=True)
            m_prev = m_scr[h:h + 1, :]
            if bias_row is not None:
                b = bias_row[:, hs]
                m_new = jnp.maximum(m_prev, blk_max + b)
                shift = m_new - b
            else:
                m_new = jnp.maximum(m_prev, blk_max)
                shift = m_new
            alpha = jnp.exp2(m_prev - m_new)
            p = jnp.exp2(s - shift)
            l_scr[h:h + 1, :] = alpha * l_scr[h:h + 1, :] + jnp.sum(p, axis=0, keepdims=True)
            acc_scr[h] = alpha * acc_scr[h] + jnp.dot(ct_blk, p.astype(BF16),
                                                      preferred_element_type=F32)
            m_scr[h:h + 1, :] = m_new

    def attend_pair(kb, **bias):
        attend(c_ref[0, pl.ds(kb, 2)].reshape(2 * QB, KV_LATENT),
               jnp.concatenate([ct_ref[0, kb], ct_ref[0, kb + 1]], axis=1),
               cap_of(keys_scr[pl.ds(kb, 2)].reshape(2 * QB, QB)), **bias)

    n_far = jnp.maximum(j - 1, 0)
    n_pairs = n_far // 2

    def far_pairs(first, count):
        for p in range(count):
            attend_pair(first + 2 * p, bias_row=bfar_ref)

    def far_body(i, carry):
        far_pairs(2 * FAR_PAIRS * i, FAR_PAIRS)
        return carry

    lax.fori_loop(0, n_pairs // FAR_PAIRS, far_body, 0)
    left = n_pairs % FAR_PAIRS
    done = 2 * (n_pairs - left)
    width = FAR_PAIRS // 2
    while width >= 1:
        @pl.when((left // width) % 2 == 1)
        def _(width=width, at=done + 2 * width * (left // (2 * width)) * 2):
            far_pairs(at, width)

        width //= 2

    @pl.when(n_far % 2 == 1)
    def _():
        attend(c_ref[0, n_far - 1], ct_ref[0, n_far - 1], cap_of(keys_scr[n_far - 1]),
               bias_row=bfar_ref)

    @pl.when(j >= 1)
    def _():
        attend_pair(j - 1, bias_tab=bnear_ref)
        attend(cm_ref[0, 0], ctm_ref[0, 0], cap_of(keysm_scr[...]), bias_row=bfar_ref)

    @pl.when(j == 0)
    def _():
        attend(c_ref[0, 0], ct_ref[0, 0], cap_of(keys_scr[0]), bias_tab=bnear_ref.at[QB:, :])
        attend(cm_ref[0, 0], ctm_ref[0, 0], cap_of(keysm_scr[...]), bias_tab=bmeta_ref)

    outs = []
    for h in range(nh):
        u = (acc_scr[h] * (1.0 / l_scr[h:h + 1, :])).astype(BF16)
        outs.append(jnp.dot(wuvt_ref[h], u, preferred_element_type=F32))
    o_t = jnp.concatenate(outs, axis=0)
    o_ref[0] = o_t.T.astype(BF16)


def _dsa_core(qlt, qit, wt, kn, c, ct, km, cm, ctm, bnear, bmeta, bfar, wuvt, *, n_sel):
    b, nkb = c.shape[:2]
    d = wuvt.shape[0] * wuvt.shape[1]
    head_dim = wuvt.shape[1]
    kern = functools.partial(_dsa_core_kernel, n_sel=n_sel, head_dim=head_dim)
    per_q = lambda r, cdim: pl.BlockSpec((1, 1, r, cdim), lambda i, j: (i, j, 0, 0))
    per_b = lambda r, cdim: pl.BlockSpec((1, nkb, r, cdim), lambda i, j: (i, 0, 0, 0),
                                         pipeline_mode=pl.Buffered(1))
    row = pl.BlockSpec((1, QB, d), lambda i, j: (i, j, 0))
    return pl.pallas_call(
        kern,
        grid=(b, nkb),
        in_specs=[per_q(KV_LATENT, ATT_HEADS * QB), per_q(IDX_DIM, IDX_HEADS * QB),
                  per_q(IDX_HEADS, QB),
                  per_b(QB, IDX_DIM), per_b(QB, KV_LATENT), per_b(KV_LATENT, QB)] +
                 [_full(a.shape) for a in (km, cm, ctm, bnear, bmeta, bfar, wuvt)],
        out_specs=row,
        out_shape=jax.ShapeDtypeStruct((b, nkb * QB, d), BF16),
        scratch_shapes=[pltpu.VMEM((nkb, QB, QB), I32),
                        pltpu.VMEM((META_ROWS, QB), I32),
                        pltpu.VMEM((nkb, QB, QB), I16),
                        pltpu.VMEM((META_ROWS, QB), I16),
                        pltpu.VMEM((ATT_HEADS, QB), F32),
                        pltpu.VMEM((ATT_HEADS, QB), F32),
                        pltpu.VMEM((ATT_HEADS, KV_LATENT, QB), F32)],
        compiler_params=_params(2),
        name="dsa_core",
    )(qlt, qit, wt, kn, c, ct, km, cm, ctm, bnear, bmeta, bfar, wuvt)


def _rel_bucket(rel):
    nb = REL_BUCKETS // 2
    max_exact = nb // 2
    ret = jnp.where(rel > 0, nb, 0)
    n = jnp.abs(rel)
    nf = jnp.maximum(n, 1).astype(F32)
    large = max_exact + (jnp.log(nf / max_exact) / math.log(REL_MAX_DIST / max_exact)
                         * (nb - max_exact)).astype(I32)
    large = jnp.minimum(large, nb - 1)
    return ret + jnp.where(n < max_exact, n, large)


def _bias_table(rel_bias, rel):
    onehot = (_rel_bucket(rel)[:, :, None] == jnp.arange(REL_BUCKETS, dtype=I32)).astype(F32)
    t = jnp.einsum("kqb,bh->khq", onehot, rel_bias.astype(F32) * LOG2_E,
                   precision=lax.Precision.HIGHEST)
    return t.reshape(rel.shape[0], -1)


def kernel(x, meta, norm_mix, norm_ffn, norm_final, gla_w_in, gla_w_a1, gla_w_a2, gla_b_a,
           gla_g_out, gla_w_out, dsa_w_in, dsa_g_kv, dsa_w_uk, dsa_w_uv, dsa_w_out, rel_bias,
           ffn_w_in, ffn_w_out):
    b, s, d = x.shape
    assert s % (2 * QB) == 0 and d % LANES == 0
    key_dim = d // 2
    head_dim = d // ATT_HEADS
    d_ff = ffn_w_out.shape[1]
    n_sel = min(TOPK_MAX, s // 4)
    assert n_sel <= QB
    row = lambda a: a.reshape(1, -1).astype(F32)
    bf = lambda a: a.astype(BF16)

    meta_x = jnp.zeros((1, META_ROWS, d), F32).at[0, :N_META].set(meta.astype(F32))
    rows = 512 if s % 512 == 0 else QB
    ffn_rows = FFN_ROWS if s % FFN_ROWS == 0 else rows

    wa1 = jnp.zeros((d, LANES), F32).at[:, :GLA_GATE_RANK].set(gla_w_a1[0])
    wa2 = jnp.zeros((LANES, key_dim), F32).at[:GLA_GATE_RANK].set(gla_w_a2[0])
    gla_args = (row(norm_mix[0]), bf(gla_w_in[0]), bf(wa1), bf(wa2), row(gla_b_a[0]))
    gout = row(gla_g_out[0])
    s0 = jnp.zeros((GLA_HEADS, d // GLA_HEADS, key_dim // GLA_HEADS), F32)
    ffn0 = (bf(gla_w_out[0]), row(norm_ffn[0]), bf(ffn_w_in[0]), bf(ffn_w_out[0]),
            row(norm_final))

    mq, mk, mv, mr, md = _gla_in(meta_x, *gla_args, rows=META_ROWS, n_valid=N_META)
    mo, s_meta = _gla_scan(mq, mk, mv, mr, md, gout, s0, rows=META_ROWS)
    h_m = _ffn(meta_x, mo, *ffn0, rows=META_ROWS, final_norm=False)

    fq, fk, fv, fr, fd = _gla_in(x, *gla_args, rows=rows, n_valid=None)
    fo, _ = _gla_scan(fq, fk, fv, fr, fd, gout, s_meta[0], rows=rows)
    h_f = _ffn(x, fo, *ffn0, rows=ffn_rows, final_norm=False)

    w = dsa_w_in[0]
    s1 = ATT_HEADS * head_dim
    s2 = s1 + KV_LATENT
    s3 = s2 + IDX_HEADS * IDX_DIM
    s4 = s3 + IDX_DIM
    dsa_args = (row(norm_mix[1]), bf(w[:, :s1].T), bf(dsa_w_uk[0]), bf(w[:, s1:s2]),
                bf(w[:, s1:s2].T), row(dsa_g_kv[0]), dsa_g_kv[0].reshape(-1, 1).astype(F32),
                bf(w[:, s2:s3].T), bf(w[:, s3:s4]), bf(w[:, s4:].T))
    _, _, cm, ctm, km, _ = _dsa_in(h_m, *dsa_args, rb=META_ROWS)
    qlt, qit, c, ct, kn, wt = _dsa_in(h_f, *dsa_args, rb=QB)

    ks = jnp.arange(2 * QB, dtype=I32)[:, None] - QB
    qs = jnp.arange(QB, dtype=I32)[None, :]
    bnear = _bias_table(rel_bias, ks - qs)
    bmeta = _bias_table(rel_bias, jnp.arange(META_ROWS, dtype=I32)[:, None] - N_META - qs)
    bfar = _bias_table(rel_bias, jnp.full((1, QB), -(QB + 1), I32))
    wuvt = bf(jnp.transpose(dsa_w_uv[0], (0, 2, 1)))
    att = _dsa_core(qlt, qit, wt, kn, c, ct, km, cm, ctm, bnear, bmeta, bfar, wuvt, n_sel=n_sel)

    ffn1 = (bf(dsa_w_out[0]), row(norm_ffn[1]), bf(ffn_w_in[1]), bf(ffn_w_out[1]),
            row(norm_final))
    return _ffn(h_f, att, *ffn1, rows=ffn_rows, final_norm=True).astype(x.dtype)
```

```python
import functools
import math

import jax
import jax.numpy as jnp
from jax import lax
from jax.experimental import pallas as pl
from jax.experimental.pallas import tpu as pltpu

F32 = jnp.float32
BF16 = jnp.bfloat16
I32 = jnp.int32
I16 = jnp.int16

EPS = 1e-6
CHUNK = 64
N_META = 16
GLA_HEADS = 4
GLA_TAU = 16.0
GLA_GATE_RANK = 16
ATT_HEADS = 8
IDX_HEADS = 8
IDX_DIM = 64
KV_LATENT = 256
TOPK_MAX = 256
REL_BUCKETS = 32
REL_MAX_DIST = 128

LANES = 128
SUBLANES = 8
PACKED_ROWS = 16
META_ROWS = 128
QB = 256
FAR_PAIRS = 4
FF_CHUNK = 256
FFN_ROWS = 1024
VMEM_LIMIT_BYTES = 56 * 1024 * 1024

INT_MIN = -(2 ** 31)
HALF_BITS = 16
HALF = 2 ** (HALF_BITS - 1)
NEG_BIG = -1e30
LAZY_L_LIMIT = 2.0 ** 100
LOG2_E = math.log2(math.e)

_NT = (((1,), (1,)), ((), ()))
_TN = (((0,), (0,)), ((), ()))


def _params(n_grid):
    return pltpu.CompilerParams(
        dimension_semantics=("arbitrary",) * n_grid,
        vmem_limit_bytes=VMEM_LIMIT_BYTES)


def _full(shape):
    nd = len(shape)
    return pl.BlockSpec(shape, lambda *_: (0,) * nd, pipeline_mode=pl.Buffered(1))


def _rms(x, gain):
    return x * lax.rsqrt(jnp.mean(x * x, axis=-1, keepdims=True) + EPS) * gain


def _silu(x):
    return x / (1.0 + jnp.exp(-x))


def _gla_in_kernel(x_ref, gn_ref, w_ref, wa1_ref, wa2_ref, ba_ref,
                   q_ref, kd_ref, v_ref, r_ref, dtot_ref,
                   *, rows, key_dim, d_model, q_scale, n_valid):
    hb = _rms(x_ref[0], gn_ref[...]).astype(BF16)
    a = jnp.dot(hb, wa1_ref[...], preferred_element_type=F32).astype(BF16)
    z = jnp.dot(a, wa2_ref[...], preferred_element_type=F32) + ba_ref[...]
    k = jnp.dot(hb, w_ref[:, key_dim:2 * key_dim], preferred_element_type=F32)
    log_sig = jnp.minimum(z, 0.0) - jnp.log(1.0 + jnp.exp(-jnp.abs(z)))
    cum = log_sig * (1.0 / GLA_TAU)
    row = lax.broadcasted_iota(I32, (rows, key_dim), 0)
    if n_valid is not None:
        cum = jnp.where(pl.program_id(1) * rows + row < n_valid, cum, 0.0)
    pos = row % CHUNK
    step = 1
    while step < CHUNK:
        cum = cum + jnp.where(pos >= step, pltpu.roll(cum, step, axis=0), 0.0)
        step *= 2
    n_chunks = rows // CHUNK
    cum = cum.reshape(n_chunks, CHUNK, key_dim)
    tot = cum[:, CHUNK - 1:CHUNK, :]
    kd = k.reshape(n_chunks, CHUNK, key_dim) * jnp.exp(tot - cum)
    kd_ref[0] = kd.reshape(rows, key_dim).astype(BF16)
    dtot_ref[0] = jnp.exp(tot).reshape(n_chunks, key_dim)
    q = jnp.dot(hb, w_ref[:, :key_dim], preferred_element_type=F32)
    q_ref[0] = (q * q_scale).astype(BF16)
    o = 2 * key_dim
    v_ref[0] = jnp.dot(hb, w_ref[:, o:o + d_model], preferred_element_type=F32).astype(BF16)
    o += d_model
    r_ref[0] = jnp.dot(hb, w_ref[:, o:o + d_model], preferred_element_type=F32).astype(BF16)


def _gla_in(x, gn, w, wa1, wa2, ba, *, rows, n_valid):
    b, s, d = x.shape
    key_dim = d // 2
    dk = key_dim // GLA_HEADS
    kern = functools.partial(_gla_in_kernel, rows=rows, key_dim=key_dim, d_model=d,
                             q_scale=dk ** -0.5, n_valid=n_valid)
    row = lambda width: pl.BlockSpec((1, rows, width), lambda i, j: (i, j, 0))
    return pl.pallas_call(
        kern,
        grid=(b, s // rows),
        in_specs=[row(d), _full(gn.shape), _full(w.shape), _full(wa1.shape),
                  _full(wa2.shape), _full(ba.shape)],
        out_specs=[row(key_dim), row(key_dim), row(d), row(d),
                   pl.BlockSpec((1, rows // CHUNK, key_dim), lambda i, j: (i, j, 0))],
        out_shape=[jax.ShapeDtypeStruct((b, s, key_dim), BF16),
                   jax.ShapeDtypeStruct((b, s, key_dim), BF16),
                   jax.ShapeDtypeStruct((b, s, d), BF16),
                   jax.ShapeDtypeStruct((b, s, d), BF16),
                   jax.ShapeDtypeStruct((b, s // CHUNK, key_dim), F32)],
        compiler_params=_params(2),
        name="gla_in",
    )(x, gn, w, wa1, wa2, ba)


def _gla_scan_kernel(q_ref, kd_ref, v_ref, r_ref, dtot_ref, gout_ref, s0_ref,
                     o_ref, sfin_ref, s_scr, *, rows, dk, dv):
    j = pl.program_id(1)

    @pl.when(j == 0)
    def _():
        s_scr[...] = s0_ref[...]

    for c in range(rows // CHUNK):
        rs = slice(c * CHUNK, (c + 1) * CHUNK)
        for h in range(GLA_HEADS):
            ks = slice(h * dk, (h + 1) * dk)
            vs = slice(h * dv, (h + 1) * dv)
            upd = lax.dot_general(v_ref[0, rs, vs], kd_ref[0, rs, ks], _TN,
                                  preferred_element_type=F32)
            s_new = dtot_ref[0, c:c + 1, ks] * s_scr[h] + upd
            s_scr[h] = s_new
            o = lax.dot_general(q_ref[0, rs, ks], s_new.astype(BF16), _NT,
                                preferred_element_type=F32)
            on = _rms(o, gout_ref[:, vs])
            rr = r_ref[0, rs, vs].astype(F32)
            o_ref[0, rs, vs] = (on * _silu(rr)).astype(BF16)

    @pl.when(j == pl.num_programs(1) - 1)
    def _():
        sfin_ref[0] = s_scr[...]


def _gla_scan(q, kd, v, r, dtot, gout, s0, *, rows):
    b, s, d = v.shape
    key_dim = q.shape[-1]
    dk, dv = key_dim // GLA_HEADS, d // GLA_HEADS
    kern = functools.partial(_gla_scan_kernel, rows=rows, dk=dk, dv=dv)
    row = lambda width: pl.BlockSpec((1, rows, width), lambda i, j: (i, j, 0))
    return pl.pallas_call(
        kern,
        grid=(b, s // rows),
        in_specs=[row(key_dim), row(key_dim), row(d), row(d),
                  pl.BlockSpec((1, rows // CHUNK, key_dim), lambda i, j: (i, j, 0)),
                  _full(gout.shape), _full(s0.shape)],
        out_specs=[row(d), pl.BlockSpec((1, GLA_HEADS, dv, dk), lambda i, j: (i, 0, 0, 0))],
        out_shape=[jax.ShapeDtypeStruct((b, s, d), BF16),
                   jax.ShapeDtypeStruct((b, GLA_HEADS, dv, dk), F32)],
        scratch_shapes=[pltpu.VMEM((GLA_HEADS, dv, dk), F32)],
        compiler_params=_params(2),
        name="gla_scan",
    )(q, kd, v, r, dtot, gout, s0)


def _ffn_kernel(h_ref, a_ref, wp_ref, gn_ref, wi_ref, wo_ref, gf_ref, o_ref, acc_scr,
                *, final_norm, d_ff):
    res = h_ref[0] + jnp.dot(a_ref[0], wp_ref[...], preferred_element_type=F32)
    hb = _rms(res, gn_ref[...]).astype(BF16)
    for t in range(d_ff // FF_CHUNK):
        lo = t * FF_CHUNK
        gate = jnp.dot(hb, wi_ref[:, lo:lo + FF_CHUNK], preferred_element_type=F32)
        up = jnp.dot(hb, wi_ref[:, d_ff + lo:d_ff + lo + FF_CHUNK], preferred_element_type=F32)
        act = (_silu(gate) * up).astype(BF16)
        part = jnp.dot(act, wo_ref[lo:lo + FF_CHUNK, :], preferred_element_type=F32)
        if t == 0:
            acc_scr[...] = res + part
        else:
            acc_scr[...] += part
    y = acc_scr[...]
    if final_norm:
        y = _rms(y, gf_ref[...])
    o_ref[0] = y


def _ffn(h, a, w_proj, gn, w_in, w_out, gf, *, rows, final_norm):
    b, s, d = h.shape
    d_ff = w_out.shape[0]
    assert d_ff % FF_CHUNK == 0
    kern = functools.partial(_ffn_kernel, final_norm=final_norm, d_ff=d_ff)
    row = pl.BlockSpec((1, rows, d), lambda i, j: (i, j, 0))
    return pl.pallas_call(
        kern,
        grid=(b, s // rows),
        in_specs=[row, row, _full(w_proj.shape), _full(gn.shape), _full(w_in.shape),
                  _full(w_out.shape), _full(gf.shape)],
        out_specs=row,
        out_shape=jax.ShapeDtypeStruct((b, s, d), F32),
        scratch_shapes=[pltpu.VMEM((rows, d), F32)],
        compiler_params=_params(2),
        name="ffn",
    )(h, a, w_proj, gn, w_in, w_out, gf)


def _dsa_in_kernel(h_ref, gn_ref, wqt_ref, wuk_ref, wc_ref, wct_ref, gkv_ref, gkvc_ref,
                   wqit_ref, wk_ref, wwt_ref,
                   qlt_ref, qit_ref, c_ref, ct_ref, k_ref, wt_ref, *, rb, head_dim):
    hb = _rms(h_ref[0], gn_ref[...]).astype(BF16)
    qt = lax.dot_general(wqt_ref[...], hb, _NT, preferred_element_type=F32)
    qit = lax.dot_general(wqit_ref[...], hb, _NT, preferred_element_type=F32)
    qit = (qit * IDX_DIM ** -0.5).astype(BF16)
    for h in range(IDX_HEADS):
        qit_ref[0, 0, :, h * rb:(h + 1) * rb] = qit[h * IDX_DIM:(h + 1) * IDX_DIM, :]
    c = jnp.dot(hb, wc_ref[...], preferred_element_type=F32)
    c_ref[0, 0] = _rms(c, gkv_ref[...]).astype(BF16)
    ct = lax.dot_general(wct_ref[...], hb, _NT, preferred_element_type=F32)
    ct = ct * lax.rsqrt(jnp.mean(ct * ct, axis=0, keepdims=True) + EPS) * gkvc_ref[...]
    ct_ref[0, 0] = ct.astype(BF16)
    k_ref[0, 0] = jnp.dot(hb, wk_ref[...], preferred_element_type=F32).astype(BF16)
    wt = lax.dot_general(wwt_ref[...], hb, _NT, preferred_element_type=F32)
    wt_ref[0, 0] = wt * IDX_HEADS ** -0.5
    for h in range(ATT_HEADS):
        qh = qt[h * head_dim:(h + 1) * head_dim, :].astype(BF16)
        ql = jnp.dot(wuk_ref[h], qh, preferred_element_type=F32)
        qlt_ref[0, 0, :, h * rb:(h + 1) * rb] = (ql * (LOG2_E * head_dim ** -0.5)).astype(BF16)


def _dsa_in(h, gn, wqt, wuk, wc, wct, gkv, gkvc, wqit, wk, wwt, *, rb):
    b, s, d = h.shape
    nb = s // rb
    head_dim = wuk.shape[-1]
    kern = functools.partial(_dsa_in_kernel, rb=rb, head_dim=head_dim)
    blk = lambda r, c: pl.BlockSpec((1, 1, r, c), lambda i, j: (i, j, 0, 0))
    shp = lambda r, c, dt: jax.ShapeDtypeStruct((b, nb, r, c), dt)
    return pl.pallas_call(
        kern,
        grid=(b, nb),
        in_specs=[pl.BlockSpec((1, rb, d), lambda i, j: (i, j, 0))] +
                 [_full(a.shape) for a in (gn, wqt, wuk, wc, wct, gkv, gkvc, wqit, wk, wwt)],
        out_specs=[blk(KV_LATENT, ATT_HEADS * rb), blk(IDX_DIM, IDX_HEADS * rb),
                   blk(rb, KV_LATENT), blk(KV_LATENT, rb), blk(rb, IDX_DIM),
                   blk(IDX_HEADS, rb)],
        out_shape=[shp(KV_LATENT, ATT_HEADS * rb, BF16), shp(IDX_DIM, IDX_HEADS * rb, BF16),
                   shp(rb, KV_LATENT, BF16), shp(KV_LATENT, rb, BF16), shp(rb, IDX_DIM, BF16),
                   shp(IDX_HEADS, rb, F32)],
        compiler_params=_params(2),
        name="dsa_in",
    )(h, gn, wqt, wuk, wc, wct, gkv, gkvc, wqit, wk, wwt)


def _sort_key(score):
    score = jnp.where(score == 0.0, 0.0, score)
    bits = pltpu.bitcast(score, I32)
    return bits ^ ((bits >> 31) & 0x7FFFFFFF)


def _dsa_core_kernel(qlt_ref, qit_ref, wt_ref, k_ref, c_ref, ct_ref,
                     km_ref, cm_ref, ctm_ref, bnear_ref, bmeta_ref, bfar_ref,
                     wuvt_ref, o_ref,
                     keys_scr, keysm_scr, k16_scr, k16m_scr, m_scr, l_scr, acc_scr,
                     *, n_sel, head_dim):
    j = pl.program_id(1)
    nh = ATT_HEADS

    def scores(k_blk):
        s8 = jnp.dot(k_blk, qit_ref[0, 0], preferred_element_type=F32)
        acc = None
        for h in range(IDX_HEADS):
            term = jnp.maximum(s8[:, h * QB:(h + 1) * QB], 0.0) * wt_ref[0, 0, h:h + 1, :]
            acc = term if acc is None else acc + term
        return _sort_key(acc)

    def hi16(keys):
        return (keys >> HALF_BITS).astype(I16)

    def score_pair(kb):
        keys = scores(k_ref[0, pl.ds(kb, 2)].reshape(2 * QB, IDX_DIM))
        keys_scr[pl.ds(kb, 2)] = keys.reshape(2, QB, QB)
        k16_scr[pl.ds(kb, 2)] = hi16(keys).reshape(2, QB, QB)

    def score_body(i, carry):
        score_pair(4 * i)
        score_pair(4 * i + 2)
        return carry

    lax.fori_loop(0, j // 4, score_body, 0)

    @pl.when((j // 2) % 2 == 1)
    def _():
        score_pair(2 * (j // 2 - 1))

    @pl.when(j % 2 == 1)
    def _():
        keys = scores(k_ref[0, j - 1])
        keys_scr[j - 1] = keys
        k16_scr[j - 1] = hi16(keys)

    s_chunk = lax.broadcasted_iota(I32, (QB, QB), 0) // CHUNK
    q_chunk = lax.broadcasted_iota(I32, (QB, QB), 1) // CHUNK
    keys = jnp.where(s_chunk <= q_chunk, scores(k_ref[0, j]), INT_MIN)
    keys_scr[j] = keys
    k16_scr[j] = hi16(keys)
    m_row = lax.broadcasted_iota(I32, (META_ROWS, QB), 0)
    keys = jnp.where(m_row < N_META, scores(km_ref[0, 0]), INT_MIN)
    keysm_scr[...] = keys
    k16m_scr[...] = hi16(keys)

    def count(blocks_ref, meta_ref, rows, dtype, hit):
        zero = jnp.zeros((rows, QB), dtype)
        one = jnp.ones((), dtype)

        def add_rows(load, n_rows, accs):
            accs = list(accs)
            for r in range(n_rows // rows):
                a = accs[r % len(accs)]
                accs[r % len(accs)] = jnp.where(hit(load(r)), a + one, a)
            return tuple(accs)

        def kb_body(kb, accs):
            return add_rows(lambda r: blocks_ref[kb, r * rows:(r + 1) * rows, :], QB, accs)

        accs = lax.fori_loop(0, j + 1, kb_body, (zero,) * 4)
        accs = add_rows(lambda r: meta_ref[r * rows:(r + 1) * rows, :], META_ROWS, accs)
        tot = (accs[0] + accs[1]) + (accs[2] + accs[3])
        return jnp.sum(tot.astype(I32), axis=0, keepdims=True)

    def to16(v):
        return jnp.broadcast_to(v, (PACKED_ROWS, QB)).astype(I16)

    def radix16(need, n_min):
        def bit_body(i, carry):
            res_u, n_ge = carry
            cand_u = res_u | jnp.left_shift(jnp.int32(1), HALF_BITS - 1 - i)
            cand = to16(cand_u - HALF)
            cnt = count(k16_scr, k16m_scr, PACKED_ROWS, I16, lambda e: e >= cand)
            keep = cnt >= need
            return jnp.where(keep, cand_u, res_u), jnp.where(keep, cnt, n_ge)

        return lax.fori_loop(0, HALF_BITS, bit_body, (jnp.zeros((1, QB), I32), n_min))

    n_entries = jnp.full((1, QB), META_ROWS, I32) + (j + 1) * QB
    hi_u, n_ge_hi = radix16(n_sel, n_entries)
    hi_sel = hi_u - HALF
    hi_sel16 = to16(hi_sel)
    above = count(k16_scr, k16m_scr, PACKED_ROWS, I16, lambda e: e > hi_sel16)

    def low_half(keys):
        lo = jnp.where((keys >> HALF_BITS) == hi_sel, (keys & (2 * HALF - 1)) - HALF, -HALF)
        return lo.astype(I16)

    def low_body(kb, carry):
        k16_scr[kb] = low_half(keys_scr[kb])
        return carry

    lax.fori_loop(0, j + 1, low_body, 0)
    k16m_scr[...] = low_half(keysm_scr[...])
    lo_sel, n_ge_lo = radix16(n_sel - above, n_ge_hi - above)
    tau = jnp.maximum(jnp.left_shift(hi_sel, HALF_BITS) | lo_sel, INT_MIN + 1)

    at_least = jnp.where(hi_u > 0, above + n_ge_lo, 0)

    @pl.when(jnp.max(at_least) > n_sel)
    def _():
        tau8 = jnp.broadcast_to(tau, (SUBLANES, QB))
        quota = n_sel - count(keys_scr, keysm_scr, SUBLANES, I32, lambda e: e > tau8)

        def demote(ref, n, seen):
            keys = ref[...]
            tie = keys == tau
            tri = (lax.broadcasted_iota(I32, (n, n), 0) >=
                   lax.broadcasted_iota(I32, (n, n), 1)).astype(BF16)
            rank = jnp.dot(tri, jnp.where(tie, 1.0, 0.0).astype(BF16),
                           preferred_element_type=F32).astype(I32)
            ref[...] = jnp.where(tie & (seen + rank > quota), tau - 1, keys)
            return seen + rank[n - 1:n, :]

        seen = demote(keysm_scr, META_ROWS, jnp.zeros((1, QB), I32))
        lax.fori_loop(0, j + 1, lambda kb, seen: demote(keys_scr.at[kb], QB, seen), seen)

    def cap_of(keys):
        return jnp.where(keys >= tau, -NEG_BIG, NEG_BIG)

    def attend(c_blk, ct_blk, cap, bias_tab=None, bias_row=None, *, lazy=False):
        for h in range(nh):
            hs = slice(h * QB, (h + 1) * QB)
            s = jnp.dot(c_blk, qlt_ref[0, 0, :, hs], preferred_element_type=F32)
            if bias_tab is not None:
                s = s + bias_tab[:, hs]
            s = jnp.minimum(s, cap)
            m_prev = m_scr[h:h + 1, :]
            b = bias_row[:, hs] if bias_row is not None else 0.0
            if lazy:
                p = jnp.exp2(s - (m_prev - b))
                l_scr[h:h + 1, :] = l_scr[h:h + 1, :] + jnp.sum(p, axis=0, keepdims=True)
                acc_scr[h] = acc_scr[h] + jnp.dot(ct_blk, p.astype(BF16),
                                                  preferred_element_type=F32)
                continue
            m_new = jnp.maximum(m_prev, jnp.max(s, axis=0, keepdims=True) + b)
            alpha = jnp.exp2(m_prev - m_new)
            p = jnp.exp2(s - (m_new - b))
            l_scr[h:h + 1, :] = alpha * l_scr[h:h + 1, :] + jnp.sum(p, axis=0, keepdims=True)
            acc_scr[h] = alpha * acc_scr[h] + jnp.dot(ct_blk, p.astype(BF16),
                                                      preferred_element_type=F32)
            m_scr[h:h + 1, :] = m_new

    def attend_pair(kb, **kw):
        attend(c_ref[0, pl.ds(kb, 2)].reshape(2 * QB, KV_LATENT),
               jnp.concatenate([ct_ref[0, kb], ct_ref[0, kb + 1]], axis=1),
               cap_of(keys_scr[pl.ds(kb, 2)].reshape(2 * QB, QB)), **kw)

    def attention(lazy):
        m_scr[...] = jnp.full(m_scr.shape, NEG_BIG, F32)
        l_scr[...] = jnp.zeros_like(l_scr)
        acc_scr[...] = jnp.zeros_like(acc_scr)

        @pl.when(j >= 1)
        def _():
            attend_pair(j - 1, bias_tab=bnear_ref)
            attend(cm_ref[0, 0], ctm_ref[0, 0], cap_of(keysm_scr[...]), bias_row=bfar_ref)

        @pl.when(j == 0)
        def _():
            attend(c_ref[0, 0], ct_ref[0, 0], cap_of(keys_scr[0]), bias_tab=bnear_ref.at[QB:, :])
            attend(cm_ref[0, 0], ctm_ref[0, 0], cap_of(keysm_scr[...]), bias_tab=bmeta_ref)

        n_far = jnp.maximum(j - 1, 0)
        n_pairs = n_far // 2

        def far_pairs(first, count):
            for p in range(count):
                attend_pair(first + 2 * p, bias_row=bfar_ref, lazy=lazy)

        def far_body(i, carry):
            far_pairs(2 * FAR_PAIRS * i, FAR_PAIRS)
            return carry

        lax.fori_loop(0, n_pairs // FAR_PAIRS, far_body, 0)
        left = n_pairs % FAR_PAIRS
        done = 2 * (n_pairs - left)
        width = FAR_PAIRS // 2
        while width >= 1:
            @pl.when((left // width) % 2 == 1)
            def _(width=width, at=done + 2 * width * (left // (2 * width)) * 2):
                far_pairs(at, width)

            width //= 2

        @pl.when(n_far % 2 == 1)
        def _():
            attend(c_ref[0, n_far - 1], ct_ref[0, n_far - 1], cap_of(keys_scr[n_far - 1]),
                   bias_row=bfar_ref, lazy=lazy)

    attention(lazy=True)

    @pl.when(jnp.logical_not(jnp.max(l_scr[...]) < LAZY_L_LIMIT))
    def _():
        attention(lazy=False)

    outs = []
    for h in range(nh):
        u = (acc_scr[h] * (1.0 / l_scr[h:h + 1, :])).astype(BF16)
        outs.append(jnp.dot(wuvt_ref[h], u, preferred_element_type=F32))
    o_t = jnp.concatenate(outs, axis=0)
    o_ref[0] = o_t.T.astype(BF16)


def _dsa_core(qlt, qit, wt, kn, c, ct, km, cm, ctm, bnear, bmeta, bfar, wuvt, *, n_sel):
    b, nkb = c.shape[:2]
    d = wuvt.shape[0] * wuvt.shape[1]
    head_dim = wuvt.shape[1]
    kern = functools.partial(_dsa_core_kernel, n_sel=n_sel, head_dim=head_dim)
    per_q = lambda r, cdim: pl.BlockSpec((1, 1, r, cdim), lambda i, j: (i, j, 0, 0))
    per_b = lambda r, cdim: pl.BlockSpec((1, nkb, r, cdim), lambda i, j: (i, 0, 0, 0),
                                         pipeline_mode=pl.Buffered(1))
    row = pl.BlockSpec((1, QB, d), lambda i, j: (i, j, 0))
    return pl.pallas_call(
        kern,
        grid=(b, nkb),
        in_specs=[per_q(KV_LATENT, ATT_HEADS * QB), per_q(IDX_DIM, IDX_HEADS * QB),
                  per_q(IDX_HEADS, QB),
                  per_b(QB, IDX_DIM), per_b(QB, KV_LATENT), per_b(KV_LATENT, QB)] +
                 [_full(a.shape) for a in (km, cm, ctm, bnear, bmeta, bfar, wuvt)],
        out_specs=row,
        out_shape=jax.ShapeDtypeStruct((b, nkb * QB, d), BF16),
        scratch_shapes=[pltpu.VMEM((nkb, QB, QB), I32),
                        pltpu.VMEM((META_ROWS, QB), I32),
                        pltpu.VMEM((nkb, QB, QB), I16),
                        pltpu.VMEM((META_ROWS, QB), I16),
                        pltpu.VMEM((ATT_HEADS, QB), F32),
                        pltpu.VMEM((ATT_HEADS, QB), F32),
                        pltpu.VMEM((ATT_HEADS, KV_LATENT, QB), F32)],
        compiler_params=_params(2),
        name="dsa_core",
    )(qlt, qit, wt, kn, c, ct, km, cm, ctm, bnear, bmeta, bfar, wuvt)


def _rel_bucket(rel):
    nb = REL_BUCKETS // 2
    max_exact = nb // 2
    ret = jnp.where(rel > 0, nb, 0)
    n = jnp.abs(rel)
    nf = jnp.maximum(n, 1).astype(F32)
    large = max_exact + (jnp.log(nf / max_exact) / math.log(REL_MAX_DIST / max_exact)
                         * (nb - max_exact)).astype(I32)
    large = jnp.minimum(large, nb - 1)
    return ret + jnp.where(n < max_exact, n, large)


def _bias_table(rel_bias, rel):
    onehot = (_rel_bucket(rel)[:, :, None] == jnp.arange(REL_BUCKETS, dtype=I32)).astype(F32)
    t = jnp.einsum("kqb,bh->khq", onehot, rel_bias.astype(F32) * LOG2_E,
                   precision=lax.Precision.HIGHEST)
    return t.reshape(rel.shape[0], -1)


def kernel(x, meta, norm_mix, norm_ffn, norm_final, gla_w_in, gla_w_a1, gla_w_a2, gla_b_a,
           gla_g_out, gla_w_out, dsa_w_in, dsa_g_kv, dsa_w_uk, dsa_w_uv, dsa_w_out, rel_bias,
           ffn_w_in, ffn_w_out):
    b, s, d = x.shape
    assert s % QB == 0 and d % LANES == 0
    key_dim = d // 2
    head_dim = d // ATT_HEADS
    d_ff = ffn_w_out.shape[1]
    n_sel = min(TOPK_MAX, s // 4)
    row = lambda a: a.reshape(1, -1).astype(F32)
    bf = lambda a: a.astype(BF16)

    meta_x = jnp.zeros((1, META_ROWS, d), F32).at[0, :N_META].set(meta.astype(F32))
    rows = 512 if s % 512 == 0 else QB
    ffn_rows = FFN_ROWS if s % FFN_ROWS == 0 else rows

    wa1 = jnp.zeros((d, LANES), F32).at[:, :GLA_GATE_RANK].set(gla_w_a1[0])
    wa2 = jnp.zeros((LANES, key_dim), F32).at[:GLA_GATE_RANK].set(gla_w_a2[0])
    gla_args = (row(norm_mix[0]), bf(gla_w_in[0]), bf(wa1), bf(wa2), row(gla_b_a[0]))
    gout = row(gla_g_out[0])
    s0 = jnp.zeros((GLA_HEADS, d // GLA_HEADS, key_dim // GLA_HEADS), F32)
    ffn0 = (bf(gla_w_out[0]), row(norm_ffn[0]), bf(ffn_w_in[0]), bf(ffn_w_out[0]),
            row(norm_final))

    mq, mk, mv, mr, md = _gla_in(meta_x, *gla_args, rows=META_ROWS, n_valid=N_META)
    mo, s_meta = _gla_scan(mq, mk, mv, mr, md, gout, s0, rows=META_ROWS)
    h_m = _ffn(meta_x, mo, *ffn0, rows=META_ROWS, final_norm=False)

    fq, fk, fv, fr, fd = _gla_in(x, *gla_args, rows=rows, n_valid=None)
    fo, _ = _gla_scan(fq, fk, fv, fr, fd, gout, s_meta[0], rows=rows)
    h_f = _ffn(x, fo, *ffn0, rows=ffn_rows, final_norm=False)

    w = dsa_w_in[0]
    s1 = ATT_HEADS * head_dim
    s2 = s1 + KV_LATENT
    s3 = s2 + IDX_HEADS * IDX_DIM
    s4 = s3 + IDX_DIM
    dsa_args = (row(norm_mix[1]), bf(w[:, :s1].T), bf(dsa_w_uk[0]), bf(w[:, s1:s2]),
                bf(w[:, s1:s2].T), row(dsa_g_kv[0]), dsa_g_kv[0].reshape(-1, 1).astype(F32),
                bf(w[:, s2:s3].T), bf(w[:, s3:s4]), bf(w[:, s4:].T))
    _, _, cm, ctm, km, _ = _dsa_in(h_m, *dsa_args, rb=META_ROWS)
    qlt, qit, c, ct, kn, wt = _dsa_in(h_f, *dsa_args, rb=QB)

    ks = jnp.arange(2 * QB, dtype=I32)[:, None] - QB
    qs = jnp.arange(QB, dtype=I32)[None, :]
    bnear = _bias_table(rel_bias, ks - qs)
    bmeta = _bias_table(rel_bias, jnp.arange(META_ROWS, dtype=I32)[:, None] - N_META - qs)
    bfar = _bias_table(rel_bias, jnp.full((1, QB), -(QB + 1), I32))
    wuvt = bf(jnp.transpose(dsa_w_uv[0], (0, 2, 1)))
    att = _dsa_core(qlt, qit, wt, kn, c, ct, km, cm, ctm, bnear, bmeta, bfar, wuvt, n_sel=n_sel)

    ffn1 = (bf(dsa_w_out[0]), row(norm_ffn[1]), bf(ffn_w_in[1]), bf(ffn_w_out[1]),
            row(norm_final))
    return _ffn(h_f, att, *ffn1, rows=ffn_rows, final_norm=True).astype(x.dtype)
```

```python
import functools
import math

import jax
import jax.numpy as jnp
from jax import lax
from jax.experimental import pallas as pl
from jax.experimental.pallas import tpu as pltpu

F32 = jnp.float32
BF16 = jnp.bfloat16
I32 = jnp.int32
I16 = jnp.int16

EPS = 1e-6
CHUNK = 64
N_META = 16
GLA_HEADS = 4
GLA_TAU = 16.0
GLA_GATE_RANK = 16
ATT_HEADS = 8
IDX_HEADS = 8
IDX_DIM = 64
KV_LATENT = 256
TOPK_MAX = 256
REL_BUCKETS = 32
REL_MAX_DIST = 128

LANES = 128
SUBLANES = 8
PACKED_ROWS = 16
META_ROWS = 128
QB = 256
FAR_PAIRS = 4
FF_CHUNK = 256
FFN_ROWS = 1024
VMEM_LIMIT_BYTES = 56 * 1024 * 1024

INT_MIN = -(2 ** 31)
HALF_BITS = 16
HALF = 2 ** (HALF_BITS - 1)
NEG_BIG = -1e30
LAZY_L_LIMIT = 2.0 ** 100
LOG2_E = math.log2(math.e)

_NT = (((1,), (1,)), ((), ()))
_TN = (((0,), (0,)), ((), ()))


def _params(n_grid):
    return pltpu.CompilerParams(
        dimension_semantics=("arbitrary",) * n_grid,
        vmem_limit_bytes=VMEM_LIMIT_BYTES)


def _full(shape):
    nd = len(shape)
    return pl.BlockSpec(shape, lambda *_: (0,) * nd, pipeline_mode=pl.Buffered(1))


def _rms(x, gain):
    return x * lax.rsqrt(jnp.mean(x * x, axis=-1, keepdims=True) + EPS) * gain


def _silu(x):
    return x / (1.0 + jnp.exp(-x))


def _gla_in_kernel(x_ref, gn_ref, w_ref, wa1_ref, wa2_ref, ba_ref,
                   q_ref, kd_ref, v_ref, r_ref, dtot_ref,
                   *, rows, key_dim, d_model, q_scale, n_valid):
    hb = _rms(x_ref[0], gn_ref[...]).astype(BF16)
    a = jnp.dot(hb, wa1_ref[...], preferred_element_type=F32).astype(BF16)
    z = jnp.dot(a, wa2_ref[...], preferred_element_type=F32) + ba_ref[...]
    k = jnp.dot(hb, w_ref[:, key_dim:2 * key_dim], preferred_element_type=F32)
    log_sig = jnp.minimum(z, 0.0) - jnp.log(1.0 + jnp.exp(-jnp.abs(z)))
    cum = log_sig * (1.0 / GLA_TAU)
    row = lax.broadcasted_iota(I32, (rows, key_dim), 0)
    if n_valid is not None:
        cum = jnp.where(pl.program_id(1) * rows + row < n_valid, cum, 0.0)
    pos = row % CHUNK
    step = 1
    while step < CHUNK:
        cum = cum + jnp.where(pos >= step, pltpu.roll(cum, step, axis=0), 0.0)
        step *= 2
    n_chunks = rows // CHUNK
    cum = cum.reshape(n_chunks, CHUNK, key_dim)
    tot = cum[:, CHUNK - 1:CHUNK, :]
    kd = k.reshape(n_chunks, CHUNK, key_dim) * jnp.exp(tot - cum)
    kd_ref[0] = kd.reshape(rows, key_dim).astype(BF16)
    dtot_ref[0] = jnp.exp(tot).reshape(n_chunks, key_dim)
    q = jnp.dot(hb, w_ref[:, :key_dim], preferred_element_type=F32)
    q_ref[0] = (q * q_scale).astype(BF16)
    o = 2 * key_dim
    v_ref[0] = jnp.dot(hb, w_ref[:, o:o + d_model], preferred_element_type=F32).astype(BF16)
    o += d_model
    r_ref[0] = jnp.dot(hb, w_ref[:, o:o + d_model], preferred_element_type=F32).astype(BF16)


def _gla_in(x, gn, w, wa1, wa2, ba, *, rows, n_valid):
    b, s, d = x.shape
    key_dim = d // 2
    dk = key_dim // GLA_HEADS
    kern = functools.partial(_gla_in_kernel, rows=rows, key_dim=key_dim, d_model=d,
                             q_scale=dk ** -0.5, n_valid=n_valid)
    row = lambda width: pl.BlockSpec((1, rows, width), lambda i, j: (i, j, 0))
    return pl.pallas_call(
        kern,
        grid=(b, s // rows),
        in_specs=[row(d), _full(gn.shape), _full(w.shape), _full(wa1.shape),
                  _full(wa2.shape), _full(ba.shape)],
        out_specs=[row(key_dim), row(key_dim), row(d), row(d),
                   pl.BlockSpec((1, rows // CHUNK, key_dim), lambda i, j: (i, j, 0))],
        out_shape=[jax.ShapeDtypeStruct((b, s, key_dim), BF16),
                   jax.ShapeDtypeStruct((b, s, key_dim), BF16),
                   jax.ShapeDtypeStruct((b, s, d), BF16),
                   jax.ShapeDtypeStruct((b, s, d), BF16),
                   jax.ShapeDtypeStruct((b, s // CHUNK, key_dim), F32)],
        compiler_params=_params(2),
        name="gla_in",
    )(x, gn, w, wa1, wa2, ba)


def _gla_scan_kernel(q_ref, kd_ref, v_ref, r_ref, dtot_ref, gout_ref, s0_ref,
                     o_ref, sfin_ref, s_scr, *, rows, dk, dv):
    j = pl.program_id(1)

    @pl.when(j == 0)
    def _():
        s_scr[...] = s0_ref[...]

    for c in range(rows // CHUNK):
        rs = slice(c * CHUNK, (c + 1) * CHUNK)
        for h in range(GLA_HEADS):
            ks = slice(h * dk, (h + 1) * dk)
            vs = slice(h * dv, (h + 1) * dv)
            upd = lax.dot_general(v_ref[0, rs, vs], kd_ref[0, rs, ks], _TN,
                                  preferred_element_type=F32)
            s_new = dtot_ref[0, c:c + 1, ks] * s_scr[h] + upd
            s_scr[h] = s_new
            o = lax.dot_general(q_ref[0, rs, ks], s_new.astype(BF16), _NT,
                                preferred_element_type=F32)
            on = _rms(o, gout_ref[:, vs])
            rr = r_ref[0, rs, vs].astype(F32)
            o_ref[0, rs, vs] = (on * _silu(rr)).astype(BF16)

    @pl.when(j == pl.num_programs(1) - 1)
    def _():
        sfin_ref[0] = s_scr[...]


def _gla_scan(q, kd, v, r, dtot, gout, s0, *, rows):
    b, s, d = v.shape
    key_dim = q.shape[-1]
    dk, dv = key_dim // GLA_HEADS, d // GLA_HEADS
    kern = functools.partial(_gla_scan_kernel, rows=rows, dk=dk, dv=dv)
    row = lambda width: pl.BlockSpec((1, rows, width), lambda i, j: (i, j, 0))
    return pl.pallas_call(
        kern,
        grid=(b, s // rows),
        in_specs=[row(key_dim), row(key_dim), row(d), row(d),
                  pl.BlockSpec((1, rows // CHUNK, key_dim), lambda i, j: (i, j, 0)),
                  _full(gout.shape), _full(s0.shape)],
        out_specs=[row(d), pl.BlockSpec((1, GLA_HEADS, dv, dk), lambda i, j: (i, 0, 0, 0))],
        out_shape=[jax.ShapeDtypeStruct((b, s, d), BF16),
                   jax.ShapeDtypeStruct((b, GLA_HEADS, dv, dk), F32)],
        scratch_shapes=[pltpu.VMEM((GLA_HEADS, dv, dk), F32)],
        compiler_params=_params(2),
        name="gla_scan",
    )(q, kd, v, r, dtot, gout, s0)


def _ffn_kernel(h_ref, a_ref, wp_ref, gn_ref, wi_ref, wo_ref, gf_ref, o_ref, acc_scr,
                *, final_norm, d_ff):
    res = h_ref[0] + jnp.dot(a_ref[0], wp_ref[...], preferred_element_type=F32)
    hb = _rms(res, gn_ref[...]).astype(BF16)
    for t in range(d_ff // FF_CHUNK):
        lo = t * FF_CHUNK
        gate = jnp.dot(hb, wi_ref[:, lo:lo + FF_CHUNK], preferred_element_type=F32)
        up = jnp.dot(hb, wi_ref[:, d_ff + lo:d_ff + lo + FF_CHUNK], preferred_element_type=F32)
        act = (_silu(gate) * up).astype(BF16)
        part = jnp.dot(act, wo_ref[lo:lo + FF_CHUNK, :], preferred_element_type=F32)
        if t == 0:
            acc_scr[...] = res + part
        else:
            acc_scr[...] += part
    y = acc_scr[...]
    if final_norm:
        y = _rms(y, gf_ref[...])
    o_ref[0] = y


def _ffn(h, a, w_proj, gn, w_in, w_out, gf, *, rows, final_norm):
    b, s, d = h.shape
    d_ff = w_out.shape[0]
    assert d_ff % FF_CHUNK == 0
    kern = functools.partial(_ffn_kernel, final_norm=final_norm, d_ff=d_ff)
    row = pl.BlockSpec((1, rows, d), lambda i, j: (i, j, 0))
    return pl.pallas_call(
        kern,
        grid=(b, s // rows),
        in_specs=[row, row, _full(w_proj.shape), _full(gn.shape), _full(w_in.shape),
                  _full(w_out.shape), _full(gf.shape)],
        out_specs=row,
        out_shape=jax.ShapeDtypeStruct((b, s, d), F32),
        scratch_shapes=[pltpu.VMEM((rows, d), F32)],
        compiler_params=_params(2),
        name="ffn",
    )(h, a, w_proj, gn, w_in, w_out, gf)


def _dsa_in_kernel(h_ref, gn_ref, wqt_ref, wuk_ref, wc_ref, wct_ref, gkv_ref, gkvc_ref,
                   wqit_ref, wk_ref, wwt_ref,
                   qlt_ref, qit_ref, c_ref, ct_ref, k_ref, wt_ref, *, rb, head_dim):
    hb = _rms(h_ref[0], gn_ref[...]).astype(BF16)
    qt = lax.dot_general(wqt_ref[...], hb, _NT, preferred_element_type=F32)
    qit = lax.dot_general(wqit_ref[...], hb, _NT, preferred_element_type=F32)
    qit = (qit * IDX_DIM ** -0.5).astype(BF16)
    for h in range(IDX_HEADS):
        qit_ref[0, 0, :, h * rb:(h + 1) * rb] = qit[h * IDX_DIM:(h + 1) * IDX_DIM, :]
    c = jnp.dot(hb, wc_ref[...], preferred_element_type=F32)
    c_ref[0, 0] = _rms(c, gkv_ref[...]).astype(BF16)
    ct = lax.dot_general(wct_ref[...], hb, _NT, preferred_element_type=F32)
    ct = ct * lax.rsqrt(jnp.mean(ct * ct, axis=0, keepdims=True) + EPS) * gkvc_ref[...]
    ct_ref[0, 0] = ct.astype(BF16)
    k_ref[0, 0] = jnp.dot(hb, wk_ref[...], preferred_element_type=F32).astype(BF16)
    wt = lax.dot_general(wwt_ref[...], hb, _NT, preferred_element_type=F32)
    wt_ref[0, 0] = wt * IDX_HEADS ** -0.5
    for h in range(ATT_HEADS):
        qh = qt[h * head_dim:(h + 1) * head_dim, :].astype(BF16)
        ql = jnp.dot(wuk_ref[h], qh, preferred_element_type=F32)
        qlt_ref[0, 0, :, h * rb:(h + 1) * rb] = (ql * (LOG2_E * head_dim ** -0.5)).astype(BF16)


def _dsa_in(h, gn, wqt, wuk, wc, wct, gkv, gkvc, wqit, wk, wwt, *, rb):
    b, s, d = h.shape
    nb = s // rb
    head_dim = wuk.shape[-1]
    kern = functools.partial(_dsa_in_kernel, rb=rb, head_dim=head_dim)
    blk = lambda r, c: pl.BlockSpec((1, 1, r, c), lambda i, j: (i, j, 0, 0))
    shp = lambda r, c, dt: jax.ShapeDtypeStruct((b, nb, r, c), dt)
    return pl.pallas_call(
        kern,
        grid=(b, nb),
        in_specs=[pl.BlockSpec((1, rb, d), lambda i, j: (i, j, 0))] +
                 [_full(a.shape) for a in (gn, wqt, wuk, wc, wct, gkv, gkvc, wqit, wk, wwt)],
        out_specs=[blk(KV_LATENT, ATT_HEADS * rb), blk(IDX_DIM, IDX_HEADS * rb),
                   blk(rb, KV_LATENT), blk(KV_LATENT, rb), blk(rb, IDX_DIM),
                   blk(IDX_HEADS, rb)],
        out_shape=[shp(KV_LATENT, ATT_HEADS * rb, BF16), shp(IDX_DIM, IDX_HEADS * rb, BF16),
                   shp(rb, KV_LATENT, BF16), shp(KV_LATENT, rb, BF16), shp(rb, IDX_DIM, BF16),
                   shp(IDX_HEADS, rb, F32)],
        compiler_params=_params(2),
        name="dsa_in",
    )(h, gn, wqt, wuk, wc, wct, gkv, gkvc, wqit, wk, wwt)


def _sort_key(score):
    score = jnp.where(score == 0.0, 0.0, score)
    bits = pltpu.bitcast(score, I32)
    return bits ^ ((bits >> 31) & 0x7FFFFFFF)


def _dsa_core_kernel(qlt_ref, qit_ref, wt_ref, k_ref, c_ref, ct_ref,
                     km_ref, cm_ref, ctm_ref, bnear_ref, bmeta_ref, bfar_ref,
                     wuvt_ref, o_ref,
                     keys_scr, keysm_scr, k16_scr, k16m_scr, m_scr, l_scr, acc_scr,
                     *, n_sel, head_dim):
    j = pl.program_id(1)
    nh = ATT_HEADS

    def scores(k_blk):
        s8 = jnp.dot(k_blk, qit_ref[0, 0], preferred_element_type=F32)
        acc = None
        for h in range(IDX_HEADS):
            term = jnp.maximum(s8[:, h * QB:(h + 1) * QB], 0.0) * wt_ref[0, 0, h:h + 1, :]
            acc = term if acc is None else acc + term
        return _sort_key(acc)

    def hi16(keys):
        return (keys >> HALF_BITS).astype(I16)

    def score_pair(kb):
        keys = scores(k_ref[0, pl.ds(kb, 2)].reshape(2 * QB, IDX_DIM))
        keys_scr[pl.ds(kb, 2)] = keys.reshape(2, QB, QB)
        k16_scr[pl.ds(kb, 2)] = hi16(keys).reshape(2, QB, QB)

    def score_body(i, carry):
        for p in range(4):
            score_pair(8 * i + 2 * p)
        return carry

    lax.fori_loop(0, j // 8, score_body, 0)

    @pl.when((j // 4) % 2 == 1)
    def _():
        score_pair(8 * (j // 8))
        score_pair(8 * (j // 8) + 2)

    @pl.when((j // 2) % 2 == 1)
    def _():
        score_pair(2 * (j // 2 - 1))

    @pl.when(j % 2 == 1)
    def _():
        keys = scores(k_ref[0, j - 1])
        keys_scr[j - 1] = keys
        k16_scr[j - 1] = hi16(keys)

    s_chunk = lax.broadcasted_iota(I32, (QB, QB), 0) // CHUNK
    q_chunk = lax.broadcasted_iota(I32, (QB, QB), 1) // CHUNK
    keys = jnp.where(s_chunk <= q_chunk, scores(k_ref[0, j]), INT_MIN)
    keys_scr[j] = keys
    k16_scr[j] = hi16(keys)
    m_row = lax.broadcasted_iota(I32, (META_ROWS, QB), 0)
    keys = jnp.where(m_row < N_META, scores(km_ref[0, 0]), INT_MIN)
    keysm_scr[...] = keys
    k16m_scr[...] = hi16(keys)

    def count(blocks_ref, meta_ref, rows, dtype, hit):
        zero = jnp.zeros((rows, QB), dtype)
        one = jnp.ones((), dtype)

        def add_rows(load, n_rows, accs):
            accs = list(accs)
            for r in range(n_rows // rows):
                a = accs[r % len(accs)]
                accs[r % len(accs)] = jnp.where(hit(load(r)), a + one, a)
            return tuple(accs)

        def kb_body(kb, accs):
            return add_rows(lambda r: blocks_ref[kb, r * rows:(r + 1) * rows, :], QB, accs)

        accs = lax.fori_loop(0, j + 1, kb_body, (zero,) * 4)
        accs = add_rows(lambda r: meta_ref[r * rows:(r + 1) * rows, :], META_ROWS, accs)
        tot = (accs[0] + accs[1]) + (accs[2] + accs[3])
        return jnp.sum(tot.astype(I32), axis=0, keepdims=True)

    def to16(v):
        return jnp.broadcast_to(v, (PACKED_ROWS, QB)).astype(I16)

    def radix16(need, n_min):
        def bit_body(i, carry):
            res_u, n_ge = carry
            cand_u = res_u | jnp.left_shift(jnp.int32(1), HALF_BITS - 1 - i)
            cand = to16(cand_u - HALF)
            cnt = count(k16_scr, k16m_scr, PACKED_ROWS, I16, lambda e: e >= cand)
            keep = cnt >= need
            return jnp.where(keep, cand_u, res_u), jnp.where(keep, cnt, n_ge)

        return lax.fori_loop(0, HALF_BITS, bit_body, (jnp.zeros((1, QB), I32), n_min))

    n_entries = jnp.full((1, QB), META_ROWS, I32) + (j + 1) * QB
    hi_u, n_ge_hi = radix16(n_sel, n_entries)
    hi_sel = hi_u - HALF
    hi_sel16 = to16(hi_sel)
    above = count(k16_scr, k16m_scr, PACKED_ROWS, I16, lambda e: e > hi_sel16)

    def low_half(keys):
        lo = jnp.where((keys >> HALF_BITS) == hi_sel, (keys & (2 * HALF - 1)) - HALF, -HALF)
        return lo.astype(I16)

    def low_body(kb, carry):
        k16_scr[kb] = low_half(keys_scr[kb])
        return carry

    lax.fori_loop(0, j + 1, low_body, 0)
    k16m_scr[...] = low_half(keysm_scr[...])
    lo_sel, n_ge_lo = radix16(n_sel - above, n_ge_hi - above)
    tau = jnp.maximum(jnp.left_shift(hi_sel, HALF_BITS) | lo_sel, INT_MIN + 1)

    at_least = jnp.where(hi_u > 0, above + n_ge_lo, 0)

    @pl.when(jnp.max(at_least) > n_sel)
    def _():
        tau8 = jnp.broadcast_to(tau, (SUBLANES, QB))
        quota = n_sel - count(keys_scr, keysm_scr, SUBLANES, I32, lambda e: e > tau8)

        def demote(ref, n, seen):
            keys = ref[...]
            tie = keys == tau
            tri = (lax.broadcasted_iota(I32, (n, n), 0) >=
                   lax.broadcasted_iota(I32, (n, n), 1)).astype(BF16)
            rank = jnp.dot(tri, jnp.where(tie, 1.0, 0.0).astype(BF16),
                           preferred_element_type=F32).astype(I32)
            ref[...] = jnp.where(tie & (seen + rank > quota), tau - 1, keys)
            return seen + rank[n - 1:n, :]

        seen = demote(keysm_scr, META_ROWS, jnp.zeros((1, QB), I32))
        lax.fori_loop(0, j + 1, lambda kb, seen: demote(keys_scr.at[kb], QB, seen), seen)

    def cap_of(keys):
        return jnp.where(keys >= tau, -NEG_BIG, NEG_BIG)

    def attend(c_blk, ct_blk, cap, bias_tab=None, bias_row=None, *, lazy=False):
        for h in range(nh):
            hs = slice(h * QB, (h + 1) * QB)
            s = jnp.dot(c_blk, qlt_ref[0, 0, :, hs], preferred_element_type=F32)
            if bias_tab is not None:
                s = s + bias_tab[:, hs]
            s = jnp.minimum(s, cap)
            m_prev = m_scr[h:h + 1, :]
            b = bias_row[:, hs] if bias_row is not None else 0.0
            if lazy:
                p = jnp.exp2(s - (m_prev - b))
                l_scr[h:h + 1, :] = l_scr[h:h + 1, :] + jnp.sum(p, axis=0, keepdims=True)
                acc_scr[h] = acc_scr[h] + jnp.dot(ct_blk, p.astype(BF16),
                                                  preferred_element_type=F32)
                continue
            m_new = jnp.maximum(m_prev, jnp.max(s, axis=0, keepdims=True) + b)
            alpha = jnp.exp2(m_prev - m_new)
            p = jnp.exp2(s - (m_new - b))
            l_scr[h:h + 1, :] = alpha * l_scr[h:h + 1, :] + jnp.sum(p, axis=0, keepdims=True)
            acc_scr[h] = alpha * acc_scr[h] + jnp.dot(ct_blk, p.astype(BF16),
                                                      preferred_element_type=F32)
            m_scr[h:h + 1, :] = m_new

    def attend_pair(kb, **kw):
        attend(c_ref[0, pl.ds(kb, 2)].reshape(2 * QB, KV_LATENT),
               jnp.concatenate([ct_ref[0, kb], ct_ref[0, kb + 1]], axis=1),
               cap_of(keys_scr[pl.ds(kb, 2)].reshape(2 * QB, QB)), **kw)

    def attention(lazy):
        m_scr[...] = jnp.full(m_scr.shape, NEG_BIG, F32)
        l_scr[...] = jnp.zeros_like(l_scr)
        acc_scr[...] = jnp.zeros_like(acc_scr)

        @pl.when(j >= 1)
        def _():
            attend_pair(j - 1, bias_tab=bnear_ref)
            attend(cm_ref[0, 0], ctm_ref[0, 0], cap_of(keysm_scr[...]), bias_row=bfar_ref,
                   lazy=lazy)

        @pl.when(j == 0)
        def _():
            attend(c_ref[0, 0], ct_ref[0, 0], cap_of(keys_scr[0]), bias_tab=bnear_ref.at[QB:, :])
            attend(cm_ref[0, 0], ctm_ref[0, 0], cap_of(keysm_scr[...]), bias_tab=bmeta_ref)

        n_far = jnp.maximum(j - 1, 0)
        n_pairs = n_far // 2

        def far_pairs(first, count):
            for p in range(count):
                attend_pair(first + 2 * p, bias_row=bfar_ref, lazy=lazy)

        def far_body(i, carry):
            far_pairs(2 * FAR_PAIRS * i, FAR_PAIRS)
            return carry

        lax.fori_loop(0, n_pairs // FAR_PAIRS, far_body, 0)
        left = n_pairs % FAR_PAIRS
        done = 2 * (n_pairs - left)
        width = FAR_PAIRS // 2
        while width >= 1:
            @pl.when((left // width) % 2 == 1)
            def _(width=width, at=done + 2 * width * (left // (2 * width)) * 2):
                far_pairs(at, width)

            width //= 2

        @pl.when(n_far % 2 == 1)
        def _():
            attend(c_ref[0, n_far - 1], ct_ref[0, n_far - 1], cap_of(keys_scr[n_far - 1]),
                   bias_row=bfar_ref, lazy=lazy)

    attention(lazy=True)

    @pl.when(jnp.logical_not(jnp.max(l_scr[...]) < LAZY_L_LIMIT))
    def _():
        attention(lazy=False)

    outs = []
    for h in range(nh):
        u = (acc_scr[h] * (1.0 / l_scr[h:h + 1, :])).astype(BF16)
        outs.append(jnp.dot(wuvt_ref[h], u, preferred_element_type=F32))
    o_t = jnp.concatenate(outs, axis=0)
    o_ref[0] = o_t.T.astype(BF16)


def _dsa_core(qlt, qit, wt, kn, c, ct, km, cm, ctm, bnear, bmeta, bfar, wuvt, *, n_sel):
    b, nkb = c.shape[:2]
    d = wuvt.shape[0] * wuvt.shape[1]
    head_dim = wuvt.shape[1]
    kern = functools.partial(_dsa_core_kernel, n_sel=n_sel, head_dim=head_dim)
    per_q = lambda r, cdim: pl.BlockSpec((1, 1, r, cdim), lambda i, j: (i, j, 0, 0))
    per_b = lambda r, cdim: pl.BlockSpec((1, nkb, r, cdim), lambda i, j: (i, 0, 0, 0),
                                         pipeline_mode=pl.Buffered(1))
    row = pl.BlockSpec((1, QB, d), lambda i, j: (i, j, 0))
    return pl.pallas_call(
        kern,
        grid=(b, nkb),
        in_specs=[per_q(KV_LATENT, ATT_HEADS * QB), per_q(IDX_DIM, IDX_HEADS * QB),
                  per_q(IDX_HEADS, QB),
                  per_b(QB, IDX_DIM), per_b(QB, KV_LATENT), per_b(KV_LATENT, QB)] +
                 [_full(a.shape) for a in (km, cm, ctm, bnear, bmeta, bfar, wuvt)],
        out_specs=row,
        out_shape=jax.ShapeDtypeStruct((b, nkb * QB, d), BF16),
        scratch_shapes=[pltpu.VMEM((nkb, QB, QB), I32),
                        pltpu.VMEM((META_ROWS, QB), I32),
                        pltpu.VMEM((nkb, QB, QB), I16),
                        pltpu.VMEM((META_ROWS, QB), I16),
                        pltpu.VMEM((ATT_HEADS, QB), F32),
                        pltpu.VMEM((ATT_HEADS, QB), F32),
                        pltpu.VMEM((ATT_HEADS, KV_LATENT, QB), F32)],
        compiler_params=_params(2),
        name="dsa_core",
    )(qlt, qit, wt, kn, c, ct, km, cm, ctm, bnear, bmeta, bfar, wuvt)


def _rel_bucket(rel):
    nb = REL_BUCKETS // 2
    max_exact = nb // 2
    ret = jnp.where(rel > 0, nb, 0)
    n = jnp.abs(rel)
    nf = jnp.maximum(n, 1).astype(F32)
    large = max_exact + (jnp.log(nf / max_exact) / math.log(REL_MAX_DIST / max_exact)
                         * (nb - max_exact)).astype(I32)
    large = jnp.minimum(large, nb - 1)
    return ret + jnp.where(n < max_exact, n, large)


def _bias_table(rel_bias, rel):
    onehot = (_rel_bucket(rel)[:, :, None] == jnp.arange(REL_BUCKETS, dtype=I32)).astype(F32)
    t = jnp.einsum("kqb,bh->khq", onehot, rel_bias.astype(F32) * LOG2_E,
                   precision=lax.Precision.HIGHEST)
    return t.reshape(rel.shape[0], -1)


def kernel(x, meta, norm_mix, norm_ffn, norm_final, gla_w_in, gla_w_a1, gla_w_a2, gla_b_a,
           gla_g_out, gla_w_out, dsa_w_in, dsa_g_kv, dsa_w_uk, dsa_w_uv, dsa_w_out, rel_bias,
           ffn_w_in, ffn_w_out):
    b, s, d = x.shape
    assert s % QB == 0 and d % LANES == 0
    key_dim = d // 2
    head_dim = d // ATT_HEADS
    d_ff = ffn_w_out.shape[1]
    n_sel = min(TOPK_MAX, s // 4)
    row = lambda a: a.reshape(1, -1).astype(F32)
    bf = lambda a: a.astype(BF16)

    meta_x = jnp.zeros((1, META_ROWS, d), F32).at[0, :N_META].set(meta.astype(F32))
    rows = 512 if s % 512 == 0 else QB
    ffn_rows = FFN_ROWS if s % FFN_ROWS == 0 else rows

    wa1 = jnp.zeros((d, LANES), F32).at[:, :GLA_GATE_RANK].set(gla_w_a1[0])
    wa2 = jnp.zeros((LANES, key_dim), F32).at[:GLA_GATE_RANK].set(gla_w_a2[0])
    gla_args = (row(norm_mix[0]), bf(gla_w_in[0]), bf(wa1), bf(wa2), row(gla_b_a[0]))
    gout = row(gla_g_out[0])
    s0 = jnp.zeros((GLA_HEADS, d // GLA_HEADS, key_dim // GLA_HEADS), F32)
    ffn0 = (bf(gla_w_out[0]), row(norm_ffn[0]), bf(ffn_w_in[0]), bf(ffn_w_out[0]),
            row(norm_final))

    mq, mk, mv, mr, md = _gla_in(meta_x, *gla_args, rows=META_ROWS, n_valid=N_META)
    mo, s_meta = _gla_scan(mq, mk, mv, mr, md, gout, s0, rows=META_ROWS)
    h_m = _ffn(meta_x, mo, *ffn0, rows=META_ROWS, final_norm=False)

    fq, fk, fv, fr, fd = _gla_in(x, *gla_args, rows=rows, n_valid=None)
    fo, _ = _gla_scan(fq, fk, fv, fr, fd, gout, s_meta[0], rows=rows)
    h_f = _ffn(x, fo, *ffn0, rows=ffn_rows, final_norm=False)

    w = dsa_w_in[0]
    s1 = ATT_HEADS * head_dim
    s2 = s1 + KV_LATENT
    s3 = s2 + IDX_HEADS * IDX_DIM
    s4 = s3 + IDX_DIM
    dsa_args = (row(norm_mix[1]), bf(w[:, :s1].T), bf(dsa_w_uk[0]), bf(w[:, s1:s2]),
                bf(w[:, s1:s2].T), row(dsa_g_kv[0]), dsa_g_kv[0].reshape(-1, 1).astype(F32),
                bf(w[:, s2:s3].T), bf(w[:, s3:s4]), bf(w[:, s4:].T))
    _, _, cm, ctm, km, _ = _dsa_in(h_m, *dsa_args, rb=META_ROWS)
    qlt, qit, c, ct, kn, wt = _dsa_in(h_f, *dsa_args, rb=QB)

    ks = jnp.arange(2 * QB, dtype=I32)[:, None] - QB
    qs = jnp.arange(QB, dtype=I32)[None, :]
    bnear = _bias_table(rel_bias, ks - qs)
    bmeta = _bias_table(rel_bias, jnp.arange(META_ROWS, dtype=I32)[:, None] - N_META - qs)
    bfar = _bias_table(rel_bias, jnp.full((1, QB), -(QB + 1), I32))
    wuvt = bf(jnp.transpose(dsa_w_uv[0], (0, 2, 1)))
    att = _dsa_core(qlt, qit, wt, kn, c, ct, km, cm, ctm, bnear, bmeta, bfar, wuvt, n_sel=n_sel)

    ffn1 = (bf(dsa_w_out[0]), row(norm_ffn[1]), bf(ffn_w_in[1]), bf(ffn_w_out[1]),
            row(norm_final))
    return _ffn(h_f, att, *ffn1, rows=ffn_rows, final_norm=True).astype(x.dtype)
```
